```python
import math
import jax
import jax.numpy as jnp
from jax import lax
import numpy as np

D_MODEL = 4096
BATCH = 4
SEQ = 2048
DEPTH = 4
DEC_BATCH = 8
DEC_SEQ = 8
PAST_LEN = 8192
PAGE_SIZE = 128

N_AB = (DEPTH + 1) // 2
N_C = DEPTH // 2
NH_A = 8
DH_A = 256
W_A = NH_A * DH_A
CHUNK_A = 64
GATE_CAP = 15.0
HEAD_EPS_A = 1e-6
HS_B = 64
W_B = D_MODEL - W_A
NH_B = W_B // HS_B
LORA_W = 96
LORA_A = 96
LORA_G = 256
N_COLS_B = 3 * W_B + LORA_W + LORA_A + LORA_G
GN_EPS_B = 64e-5
N_IN_A = 4 * W_A + 2 * NH_A
N_IN_AB = N_IN_A + N_COLS_B
NH_C = 32
DH_C = D_MODEL // NH_C
Q_BLOCK = 128
N_IN_C = 3 * D_MODEL + NH_C
D_FF = ((8 * D_MODEL + 3 * 256 - 1) // (3 * 256)) * 256
D_PLE = 256
ALPHA = (2 * DEPTH) ** 0.25
BETA = (8 * DEPTH) ** -0.25
LN_EPS = 1e-5

kernel_name = 'hybrid_mlstm_rwkv7_fox_deepnorm_step'


def layer_norm(x, g, b):
    xf = x.astype(jnp.float32)
    mu = jnp.mean(xf, -1, keepdims=True)
    var = jnp.mean(jnp.square(xf - mu), -1, keepdims=True)
    return ((xf - mu) * lax.rsqrt(var + LN_EPS)).astype(x.dtype) * g + b


def head_norm(h, eps):
    hf = h.astype(jnp.float32)
    mu = jnp.mean(hf, -1, keepdims=True)
    var = jnp.mean(jnp.square(hf - mu), -1, keepdims=True)
    return (hf - mu) * lax.rsqrt(var + eps)


def mlstm_chunkwise(q, k, v, ig, lf, c0, n0, m0):
    b_, t_, h_, d_ = q.shape
    L = math.gcd(t_, CHUNK_A)
    nc = t_ // L

    def chunks(a):
        return jnp.moveaxis(a.reshape((b_, nc, L) + a.shape[2:]), 1, 0)

    causal = jnp.tril(jnp.ones((L, L), dtype=bool))[None, :, :, None]

    def step(carry, inp):
        c, n, m = carry
        qc, kc, vc, igc, lfc = inp
        bcum = jnp.cumsum(lfc, axis=1)
        inter = bcum + m[:, None, :]
        logd = bcum[:, :, None, :] - bcum[:, None, :, :] + igc[:, None, :, :]
        logd = jnp.where(causal, logd, -jnp.inf)
        m_t = jnp.maximum(inter, jnp.max(logd, axis=2))
        w_carry = jnp.exp(inter - m_t)
        s = jnp.einsum('bthd,bshd->btsh', qc, kc) * jnp.exp(logd - m_t[:, :, None, :])
        num = jnp.einsum('btsh,bshd->bthd', s, vc) + w_carry[..., None] * jnp.einsum('bthd,bhde->bthe', qc, c)
        den = jnp.sum(s, axis=2) + w_carry * jnp.einsum('bthd,bhd->bth', qc, n)
        h = num / jnp.maximum(jnp.abs(den), jnp.exp(-m_t))[..., None]
        m_new = m_t[:, -1]
        w_c = jnp.exp(inter[:, -1] - m_new)
        w_rows = jnp.exp(bcum[:, -1:] - bcum + igc - m_new[:, None, :])
        c_new = w_c[..., None, None] * c + jnp.einsum('bsh,bshd,bshe->bhde', w_rows, kc, vc)
        n_new = w_c[..., None] * n + jnp.einsum('bsh,bshd->bhd', w_rows, kc)
        return (c_new, n_new, m_new), h

    (c1, n1, m1), hs = lax.scan(step, (c0, n0, m0), tuple(chunks(a) for a in (q, k, v, ig, lf)))
    return jnp.moveaxis(hs, 0, 1).reshape(b_, t_, h_, d_), c1, n1, m1


def rwkv7_mix(zb, shift0, mu, w0, w2, a0, a2, g2, kk_scale, k_a, r_k, gn_g, gn_b, s0):
    b_, t_, _ = zb.shape
    f32 = jnp.float32
    z_prev = jnp.concatenate([shift0[:, None, :].astype(zb.dtype), zb[:, :-1]], axis=1)
    zm = zb + mu * (z_prev - zb)
    r, k, v, wl, al, gl = jnp.split(zm, [W_B, 2 * W_B, 3 * W_B, 3 * W_B + LORA_W, 3 * W_B + LORA_W + LORA_A], axis=-1)
    w_log = -jax.nn.softplus(-(w0 + jnp.tanh(wl) @ w2).astype(f32)) - 0.5
    decay = jnp.exp(-jnp.exp(w_log))
    a = jax.nn.sigmoid((a0 + al @ a2).astype(f32))
    g = jax.nn.sigmoid(gl) @ g2

    def heads(t):
        return t.astype(f32).reshape(b_, t_, NH_B, HS_B)

    r, k, v, decay, a = (heads(t) for t in (r, k, v, decay, a))
    kk = k * kk_scale.reshape(NH_B, HS_B)
    kk = kk * lax.rsqrt(jnp.maximum(jnp.sum(kk * kk, -1, keepdims=True), 1e-24))
    k = k * (1.0 + (a - 1.0) * k_a.reshape(NH_B, HS_B))

    def step(s, inp):
        r_t, w_t, k_t, v_t, kk_t, a_t = inp
        sk = jnp.einsum('bhij,bhj->bhi', s, kk_t)
        s = s * w_t[:, :, None, :] - sk[..., None] * (kk_t * a_t)[:, :, None, :] + v_t[..., None] * k_t[:, :, None, :]
        return s, jnp.einsum('bhij,bhj->bhi', s, r_t)

    s1, ys = lax.scan(step, s0.astype(f32), tuple(jnp.moveaxis(t, 1, 0) for t in (r, decay, k, v, kk, a)))
    y = jnp.moveaxis(ys, 0, 1)
    y = head_norm(y, GN_EPS_B) * gn_g.reshape(NH_B, HS_B) + gn_b.reshape(NH_B, HS_B)
    y = y + jnp.sum(r * k * r_k, -1, keepdims=True) * v
    out = (y.reshape(b_, t_, W_B) * g).astype(zb.dtype)
    return out, s1, zb[:, -1]


def even_mixer(x, c0, n0, m0, s0, shift0, w_in, b_gate, g_head, mu, w0, w2, a0, a2, g2, kk_scale, k_a, r_k, gn_g, gn_b, w_out):
    b_, t_, _ = x.shape
    f32 = jnp.float32
    z = x @ w_in
    q, k, v, o = (z[..., i * W_A:(i + 1) * W_A].reshape(b_, t_, NH_A, DH_A) for i in range(4))
    gates = (z[..., 4 * W_A:N_IN_A] + b_gate).astype(f32)
    gates = GATE_CAP * jnp.tanh(gates / GATE_CAP)
    h, c1, n1, m1 = mlstm_chunkwise(q.astype(f32), k.astype(f32) * DH_A ** -0.5, v.astype(f32),
                                    gates[..., :NH_A], jax.nn.log_sigmoid(gates[..., NH_A:]),
                                    c0.astype(f32), n0.astype(f32), m0.astype(f32))
    h = (head_norm(h, HEAD_EPS_A) * g_head * jax.nn.sigmoid(o.astype(f32))).astype(x.dtype)
    yb, s1, shift1 = rwkv7_mix(z[..., N_IN_A:], shift0, mu, w0, w2, a0, a2, g2, kk_scale, k_a, r_k, gn_g, gn_b, s0)
    out = jnp.concatenate([h.reshape(b_, t_, W_A), yb], axis=-1) @ w_out
    return out, c1, n1, m1, s1, shift1


def fox_project(x, w_in, b_f):
    b_, t_, _ = x.shape
    z = x @ w_in
    q, k, v = (z[..., i * D_MODEL:(i + 1) * D_MODEL].reshape(b_, t_, NH_C, DH_C) for i in range(3))
    lf = jax.nn.log_sigmoid((z[..., 3 * D_MODEL:] + b_f).astype(jnp.float32))
    return q, k, v, lf


def fox_attend(q, gq, q_pos, k, gk, k_pos, v):
    b_, tq, h_, d_ = q.shape
    L = math.gcd(tq, Q_BLOCK)
    nb = tq // L
    gk_t = jnp.swapaxes(gk, 1, 2)

    def block(args):
        qb, gqb, pb = args
        s = jnp.einsum('bqhd,bkhd->bhqk', qb, k).astype(jnp.float32) * DH_C ** -0.5
        s = s + jnp.swapaxes(gqb, 1, 2)[..., None] - gk_t[:, :, None, :]
        s = jnp.where(k_pos[None, :] <= pb[:, None], s, -jnp.inf)
        pr = jax.nn.softmax(s, axis=-1).astype(v.dtype)
        return jnp.einsum('bhqk,bkhd->bqhd', pr, v)

    blocks = (jnp.moveaxis(q.reshape(b_, nb, L, h_, d_), 1, 0),
              jnp.moveaxis(gq.reshape(b_, nb, L, h_), 1, 0),
              q_pos.reshape(nb, L))
    out = lax.map(block, blocks)
    return jnp.moveaxis(out, 0, 1).reshape(b_, tq, h_, d_)


def swiglu(x, w_in, w_out):
    gate, up = jnp.split(x @ w_in, 2, axis=-1)
    return (jax.nn.silu(gate) * up) @ w_out


def post_block(x, mix_out, p_i, ln_mix_g, ln_mix_b, ln_ffn_g, ln_ffn_b, w_ffn_in, w_ffn_out, w_ple, w_ple_gate):
    x = layer_norm(ALPHA * x + mix_out, ln_mix_g, ln_mix_b)
    x = layer_norm(ALPHA * x + swiglu(x, w_ffn_in, w_ffn_out), ln_ffn_g, ln_ffn_b)
    return x + jax.nn.sigmoid(x @ w_ple_gate) * (p_i @ w_ple)


def setup_inputs(seed: int = 0) -> dict:
    key = jax.random.key(seed)
    ks = iter(jax.random.split(key, 64))

    def nrm(shape, scale=1.0):
        return scale * jax.random.normal(next(ks), shape, jnp.float32)

    def uni(shape, lo, hi):
        return jax.random.uniform(next(ks), shape, jnp.float32, lo, hi)

    n_pages = PAST_LEN // PAGE_SIZE
    n_used = DEC_BATCH * n_pages
    n_pool = n_used + (n_used + 3) // 4
    page_table = jax.random.permutation(next(ks), n_pool)[:n_used].reshape(DEC_BATCH, n_pages).astype(jnp.int32)
    d_in = D_MODEL ** -0.5
    return {
        'x_prompt': nrm((BATCH, SEQ, D_MODEL)),
        'x_sample': nrm((DEC_BATCH, DEC_SEQ, D_MODEL)),
        'cache_k': nrm((N_C, n_pool, PAGE_SIZE, NH_C, DH_C)),
        'cache_v': nrm((N_C, n_pool, PAGE_SIZE, NH_C, DH_C)),
        'cache_logf': jax.nn.log_sigmoid(nrm((N_C, n_pool, PAGE_SIZE, NH_C)) + 3.0),
        'state_mlstm_c': nrm((N_AB, DEC_BATCH, NH_A, DH_A, DH_A)),
        'state_mlstm_n': nrm((N_AB, DEC_BATCH, NH_A, DH_A)),
        'state_mlstm_m': nrm((N_AB, DEC_BATCH, NH_A)),
        'state_rwkv_wkv': nrm((N_AB, DEC_BATCH, NH_B, HS_B, HS_B), 0.5),
        'state_rwkv_shift': nrm((N_AB, DEC_BATCH, N_COLS_B)),
        'page_table': page_table,
        'p_prompt': nrm((DEPTH, BATCH, SEQ, D_PLE)),
        'p_sample': nrm((DEPTH, DEC_BATCH, DEC_SEQ, D_PLE)),
        'w_in_ab': nrm((N_AB, D_MODEL, N_IN_AB), d_in),
        'b_gate_a': jnp.concatenate([nrm((N_AB, NH_A), 0.5) - 3.0, uni((N_AB, NH_A), 3.0, 6.0)], axis=-1),
        'g_head_a': 1.0 + nrm((N_AB, NH_A, DH_A), 0.02),
        'mu_b': uni((N_AB, N_COLS_B), 0.0, 1.0),
        'w0_b': uni((N_AB, W_B), -6.0, -1.0),
        'w2_b': nrm((N_AB, LORA_W, W_B), 0.1 * LORA_W ** -0.5),
        'a0_b': nrm((N_AB, W_B), 0.1),
        'a2_b': nrm((N_AB, LORA_A, W_B), 0.5 * LORA_A ** -0.5),
        'g2_b': nrm((N_AB, LORA_G, W_B), LORA_G ** -0.5),
        'kk_scale_b': 0.85 + nrm((N_AB, W_B), 0.05),
        'ka_b': 1.0 + nrm((N_AB, W_B), 0.05),
        'rk_b': nrm((N_AB, NH_B, HS_B), 0.1),
        'gn_g_b': 1.0 + nrm((N_AB, W_B), 0.02),
        'gn_b_b': nrm((N_AB, W_B), 0.02),
        'w_out_ab': nrm((N_AB, W_A + W_B, D_MODEL), BETA * (W_A + W_B) ** -0.5),
        'w_in_c': nrm((N_C, D_MODEL, N_IN_C), d_in),
        'b_f_c': uni((N_C, NH_C), 2.0, 5.0),
        'w_out_c': nrm((N_C, D_MODEL, D_MODEL), BETA * d_in),
        'ln_mix_g': 1.0 + nrm((DEPTH, D_MODEL), 0.02),
        'ln_mix_b': nrm((DEPTH, D_MODEL), 0.02),
        'ln_ffn_g': 1.0 + nrm((DEPTH, D_MODEL), 0.02),
        'ln_ffn_b': nrm((DEPTH, D_MODEL), 0.02),
        'w_ffn_in': nrm((DEPTH, D_MODEL, 2 * D_FF), d_in),
        'w_ffn_out': nrm((DEPTH, D_FF, D_MODEL), BETA * D_FF ** -0.5),
        'w_ple': nrm((DEPTH, D_PLE, D_MODEL), D_PLE ** -0.5),
        'w_ple_gate': nrm((DEPTH, D_MODEL, D_MODEL), d_in),
    }


def reference(x_prompt, x_sample, cache_k, cache_v, cache_logf, state_mlstm_c, state_mlstm_n, state_mlstm_m,
              state_rwkv_wkv, state_rwkv_shift, page_table, p_prompt, p_sample, w_in_ab, b_gate_a, g_head_a, mu_b,
              w0_b, w2_b, a0_b, a2_b, g2_b, kk_scale_b, ka_b, rk_b, gn_g_b, gn_b_b, w_out_ab, w_in_c, b_f_c, w_out_c,
              ln_mix_g, ln_mix_b, ln_ffn_g, ln_ffn_b, w_ffn_in, w_ffn_out, w_ple, w_ple_gate):
    f32 = jnp.float32
    xp, xs = x_prompt, x_sample
    bp, tp = xp.shape[0], xp.shape[1]
    bs, ts = xs.shape[0], xs.shape[1]
    kp_l, vp_l, lfp_l, ks_l, vs_l, lfs_l = [], [], [], [], [], []
    cp_l, np_l, mp_l, cs_l, ns_l, ms_l = [], [], [], [], [], []
    sp_l, shp_l, ss_l, shs_l = [], [], [], []
    for i in range(DEPTH):
        j = i // 2
        if i % 2 == 0:
            ab = (w_in_ab[j], b_gate_a[j], g_head_a[j], mu_b[j], w0_b[j], w2_b[j], a0_b[j], a2_b[j], g2_b[j],
                  kk_scale_b[j], ka_b[j], rk_b[j], gn_g_b[j], gn_b_b[j], w_out_ab[j])
            mix_p, cp, np_, mp, sp, shp = even_mixer(
                xp, jnp.zeros((bp, NH_A, DH_A, DH_A), f32), jnp.zeros((bp, NH_A, DH_A), f32),
                jnp.zeros((bp, NH_A), f32), jnp.zeros((bp, NH_B, HS_B, HS_B), f32),
                jnp.zeros((bp, N_COLS_B), xp.dtype), *ab)
            mix_s, cs, ns, ms, ss, shs = even_mixer(
                xs, state_mlstm_c[j], state_mlstm_n[j], state_mlstm_m[j], state_rwkv_wkv[j], state_rwkv_shift[j], *ab)
            cp_l.append(cp); np_l.append(np_); mp_l.append(mp); sp_l.append(sp); shp_l.append(shp)
            cs_l.append(cs); ns_l.append(ns); ms_l.append(ms); ss_l.append(ss); shs_l.append(shs)
        else:
            qp, kp, vp, lfp = fox_project(xp, w_in_c[j], b_f_c[j])
            fp = jnp.cumsum(lfp, axis=1)
            pos_p = jnp.arange(tp)
            mix_p = fox_attend(qp, fp, pos_p, kp, fp, pos_p, vp).reshape(bp, tp, D_MODEL) @ w_out_c[j]
            qs, kn, vn, lfs = fox_project(xs, w_in_c[j], b_f_c[j])
            past_k = cache_k[j][page_table].reshape(bs, -1, NH_C, DH_C)
            past_v = cache_v[j][page_table].reshape(bs, -1, NH_C, DH_C)
            past_lf = cache_logf[j][page_table].reshape(bs, -1, NH_C).astype(f32)
            n_past = past_k.shape[1]
            g_past = past_lf - lax.cumsum(past_lf, axis=1, reverse=True)
            g_new = jnp.cumsum(lfs, axis=1)
            mix_s = fox_attend(qs, g_new, n_past + jnp.arange(ts),
                               jnp.concatenate([past_k, kn], axis=1), jnp.concatenate([g_past, g_new], axis=1),
                               jnp.arange(n_past + ts), jnp.concatenate([past_v, vn], axis=1)
                               ).reshape(bs, ts, D_MODEL) @ w_out_c[j]
            kp_l.append(kp); vp_l.append(vp); lfp_l.append(lfp)
            ks_l.append(kn); vs_l.append(vn); lfs_l.append(lfs)
        shared = (ln_mix_g[i], ln_mix_b[i], ln_ffn_g[i], ln_ffn_b[i], w_ffn_in[i], w_ffn_out[i], w_ple[i], w_ple_gate[i])
        xp = post_block(xp, mix_p, p_prompt[i], *shared)
        xs = post_block(xs, mix_s, p_sample[i], *shared)
    return (xp, xs,
            jnp.stack(kp_l), jnp.stack(vp_l), jnp.stack(lfp_l),
            jnp.stack(ks_l), jnp.stack(vs_l), jnp.stack(lfs_l),
            jnp.stack(cp_l), jnp.stack(np_l), jnp.stack(mp_l),
            jnp.stack(cs_l), jnp.stack(ns_l), jnp.stack(ms_l),
            jnp.stack(sp_l), jnp.stack(shp_l),
            jnp.stack(ss_l), jnp.stack(shs_l))
```

```python
import functools
import math

import jax
import jax.numpy as jnp
from jax import lax
from jax.experimental import pallas as pl
from jax.experimental.pallas import tpu as pltpu

F32 = jnp.float32
BF16 = jnp.bfloat16
HIGHEST = lax.Precision.HIGHEST

LANES = 128
VMEM_LIMIT_BYTES = 56 * 1024 * 1024

CHUNK_A = 64
GATE_CAP = 15.0
HEAD_EPS_A = 1e-6
CHUNK_B = 64
GN_EPS_B = 64e-5
LN_EPS = 1e-5
NEG_INF = float("-inf")


def _tile(dim, prefs):
    for t in prefs:
        if t <= dim and dim % t == 0:
            return t
    return dim


def _params(n_axes):
    return pltpu.CompilerParams(dimension_semantics=("arbitrary",) * n_axes,
                                vmem_limit_bytes=VMEM_LIMIT_BYTES)


def _dot(a, b):
    return jnp.dot(a, b, preferred_element_type=F32)


def _dot_nt(a, b):
    return lax.dot_general(a, b, (((1,), (1,)), ((), ())), preferred_element_type=F32)


def _dot_tn(a, b):
    return lax.dot_general(a, b, (((0,), (0,)), ((), ())), preferred_element_type=F32)


def _div_pow2(x, n):
    assert n & (n - 1) == 0
    return lax.shift_right_logical(x, n.bit_length() - 1)


def _mod_pow2(x, n):
    assert n & (n - 1) == 0
    return jnp.bitwise_and(x, n - 1)


def _log_sigmoid(x):
    return jnp.minimum(x, 0.0) - jnp.log1p(jnp.exp(-jnp.abs(x)))


def _softplus(x):
    return jnp.maximum(x, 0.0) + jnp.log1p(jnp.exp(-jnp.abs(x)))


def _mm_kernel(x_ref, w_ref, o_ref):
    o_ref[...] = _dot(x_ref[...], w_ref[...]).astype(o_ref.dtype)


def _mm(x, w, out_dtype):
    m, k = x.shape
    n = w.shape[1]
    tm = _tile(m, (1024, 512, 256, 128, 64, 32, 16, 8))
    tn = _tile(n, (512, 384, 256, 128))
    return pl.pallas_call(
        _mm_kernel,
        grid=(m // tm, n // tn),
        in_specs=[pl.BlockSpec((tm, k), lambda i, j: (i, 0)),
                  pl.BlockSpec((k, tn), lambda i, j: (0, j))],
        out_specs=pl.BlockSpec((tm, tn), lambda i, j: (i, j)),
        out_shape=jax.ShapeDtypeStruct((m, n), out_dtype),
        compiler_params=_params(2),
        name="mm",
    )(x, w)


def _mm2_kernel(x1_ref, x2_ref, w1_ref, w2_ref, o_ref):
    o_ref[...] = _dot(x1_ref[...], w1_ref[...]) + _dot(x2_ref[...], w2_ref[...])


def _mm2(x1, x2, w):
    m, k = x1.shape
    assert x2.shape == (m, k) and w.shape[0] == 2 * k
    n = w.shape[1]
    tm = _tile(m, (1024, 512, 256, 128, 64, 32, 16, 8))
    tn = _tile(n, (512, 384, 256, 128))
    return pl.pallas_call(
        _mm2_kernel,
        grid=(m // tm, n // tn),
        in_specs=[pl.BlockSpec((tm, k), lambda i, j: (i, 0)),
                  pl.BlockSpec((tm, k), lambda i, j: (i, 0)),
                  pl.BlockSpec((k, tn), lambda i, j: (0, j)),
                  pl.BlockSpec((k, tn), lambda i, j: (1, j))],
        out_specs=pl.BlockSpec((tm, tn), lambda i, j: (i, j)),
        out_shape=jax.ShapeDtypeStruct((m, n), F32),
        compiler_params=_params(2),
        name="mm2",
    )(x1, x2, w, w)


def _add_ln_kernel(x_ref, y_ref, g_ref, b_ref, o_ref, ob_ref, *, alpha):
    v = alpha * x_ref[...] + y_ref[...]
    mu = jnp.mean(v, axis=-1, keepdims=True)
    d = v - mu
    var = jnp.mean(d * d, axis=-1, keepdims=True)
    out = d * lax.rsqrt(var + LN_EPS) * g_ref[...] + b_ref[...]
    o_ref[...] = out
    ob_ref[...] = out.astype(BF16)


def _add_ln(x, y, g, b, alpha):
    m, d = x.shape
    tm = _tile(m, (256, 128, 64, 32, 16))
    row = pl.BlockSpec((tm, d), lambda i: (i, 0))
    vec = pl.BlockSpec((1, d), lambda i: (0, 0))
    return pl.pallas_call(
        functools.partial(_add_ln_kernel, alpha=alpha),
        grid=(m // tm,),
        in_specs=[row, row, vec, vec],
        out_specs=[row, row],
        out_shape=[jax.ShapeDtypeStruct((m, d), F32), jax.ShapeDtypeStruct((m, d), BF16)],
        compiler_params=_params(1),
        name="add_ln",
    )(x, y, g.reshape(1, d), b.reshape(1, d))


def _ffn_kernel(x_ref, wg_ref, wu_ref, wo_ref, o_ref):
    f = pl.program_id(1)
    x = x_ref[...]
    gate = _dot(x, wg_ref[...])
    up = _dot(x, wu_ref[...])
    h = (gate * jax.nn.sigmoid(gate) * up).astype(BF16)
    part = _dot(h, wo_ref[...])

    @pl.when(f == 0)
    def _():
        o_ref[...] = part

    @pl.when(f > 0)
    def _():
        o_ref[...] += part


def _ffn(xb, w_in, w_out):
    m, d = xb.shape
    f = w_out.shape[0]
    tm = _tile(m, (512, 256, 128, 64, 32, 16))
    tf = _tile(f, (256, 128))
    nf = f // tf
    return pl.pallas_call(
        _ffn_kernel,
        grid=(m // tm, nf),
        in_specs=[pl.BlockSpec((tm, d), lambda i, j: (i, 0)),
                  pl.BlockSpec((d, tf), lambda i, j: (0, j)),
                  pl.BlockSpec((d, tf), lambda i, j: (0, j + nf)),
                  pl.BlockSpec((tf, d), lambda i, j: (j, 0))],
        out_specs=pl.BlockSpec((tm, d), lambda i, j: (i, 0)),
        out_shape=jax.ShapeDtypeStruct((m, d), F32),
        compiler_params=_params(2),
        name="ffn",
    )(xb, w_in, w_in, w_out)


def _ple_kernel(xb_ref, wg_ref, pb_ref, wp_ref, x_ref, o_ref, ob_ref):
    gate = jax.nn.sigmoid(_dot(xb_ref[...], wg_ref[...]))
    out = x_ref[...] + gate * _dot(pb_ref[...], wp_ref[...])
    o_ref[...] = out
    ob_ref[...] = out.astype(BF16)


def _ple(x, xb, pb, w_gate, w_ple):
    m, d = x.shape
    dp = pb.shape[1]
    tm = _tile(m, (1024, 512, 256, 128, 64, 32, 16))
    tn = _tile(d, (512, 256, 128))
    tile = pl.BlockSpec((tm, tn), lambda i, j: (i, j))
    return pl.pallas_call(
        _ple_kernel,
        grid=(m // tm, d // tn),
        in_specs=[pl.BlockSpec((tm, d), lambda i, j: (i, 0)),
                  pl.BlockSpec((d, tn), lambda i, j: (0, j)),
                  pl.BlockSpec((tm, dp), lambda i, j: (i, 0)),
                  pl.BlockSpec((dp, tn), lambda i, j: (0, j)),
                  tile],
        out_specs=[tile, tile],
        out_shape=[jax.ShapeDtypeStruct((m, d), F32), jax.ShapeDtypeStruct((m, d), BF16)],
        compiler_params=_params(2),
        name="ple",
    )(xb, w_gate, pb, w_ple, x)


def _mlstm_gate_kernel(z_ref, b_ref, o_ref, *, nh):
    g = GATE_CAP * jnp.tanh((z_ref[...] + b_ref[...]) / GATE_CAP)
    lane = lax.broadcasted_iota(jnp.int32, g.shape, 1)
    o_ref[...] = jnp.where(lane < nh, g, _log_sigmoid(g))


def _mlstm_gates(zs, b_gate, nh):
    m = zs.shape[0]
    tm = _tile(m, (1024, 512, 256, 128, 64, 32, 16, 8))
    bpad = jnp.zeros((1, LANES), F32).at[0, :2 * nh].set(b_gate)
    return pl.pallas_call(
        functools.partial(_mlstm_gate_kernel, nh=nh),
        grid=(m // tm,),
        in_specs=[pl.BlockSpec((tm, LANES), lambda i: (i, 0)),
                  pl.BlockSpec((1, LANES), lambda i: (0, 0))],
        out_specs=pl.BlockSpec((tm, LANES), lambda i: (i, 0)),
        out_shape=jax.ShapeDtypeStruct((m, LANES), F32),
        compiler_params=_params(1),
        name="mlstm_gates",
    )(zs, bpad)


def _mlstm_kernel(q_ref, k_ref, v_ref, o_ref, gc_ref, gr_ref, gh_ref, c0_ref, n0_ref, m0_ref,
                  h_ref, c1_ref, n1_ref, m1_ref, c_s, n_s, m_s, *, chunk, n_chunks, scale):
    tb = pl.program_id(2)

    @pl.when(tb == 0)
    def _():
        c_s[...] = c0_ref[0, 0]
        n_s[...] = n0_ref[0, 0]
        m_s[...] = m0_ref[0, 0]

    L = chunk
    ti = lax.broadcasted_iota(jnp.int32, (L, L), 0)
    si = lax.broadcasted_iota(jnp.int32, (L, L), 1)
    causal = si <= ti
    gc = gc_ref[0, 0]
    gr = gr_ref[0, 0]
    c = c_s[...]
    n = n_s[...]
    m = m_s[...]
    for ci in range(n_chunks):
        sl = slice(ci * L, (ci + 1) * L)
        q = q_ref[sl, :].astype(BF16)
        ks = k_ref[sl, :].astype(F32) * scale
        v = v_ref[sl, :].astype(BF16)
        ig_col, lf_col = gc[sl, 0:1], gc[sl, 1:2]
        ig_row, lf_row = gr[0:1, sl], gr[1:2, sl]
        bcum_col = jnp.sum(jnp.where(causal, lf_row, 0.0), axis=1, keepdims=True)
        bcum_row = jnp.sum(jnp.where(ti <= si, lf_col, 0.0), axis=0, keepdims=True)
        inter = bcum_col + m
        logd = jnp.where(causal, bcum_col - bcum_row + ig_row, NEG_INF)
        m_t = jnp.maximum(inter, jnp.max(logd, axis=1, keepdims=True))
        w_carry = jnp.exp(inter - m_t)
        s = _dot_nt(q, ks.astype(BF16)) * jnp.exp(logd - m_t)
        num = _dot(s.astype(BF16), v) + w_carry * _dot(q, c.astype(BF16))
        qn = jnp.sum(q.astype(F32) * n, axis=1, keepdims=True)
        den = jnp.sum(s, axis=1, keepdims=True) + w_carry * qn
        h = num / jnp.maximum(jnp.abs(den), jnp.exp(-m_t))
        m_new = m_t[L - 1:L, :]
        w_c = jnp.exp(inter[L - 1:L, :] - m_new)
        w_rows = jnp.exp(bcum_col[L - 1:L, :] - bcum_col + ig_col - m_new)
        kw = w_rows * ks
        c = w_c * c + _dot_tn(kw.astype(BF16), v)
        n = w_c * n + jnp.sum(kw, axis=0, keepdims=True)
        m = m_new
        mu = jnp.mean(h, axis=-1, keepdims=True)
        d = h - mu
        var = jnp.mean(d * d, axis=-1, keepdims=True)
        hn = d * lax.rsqrt(var + HEAD_EPS_A) * gh_ref[0] * jax.nn.sigmoid(o_ref[sl, :].astype(F32))
        h_ref[sl, :] = hn.astype(h_ref.dtype)
    c_s[...] = c
    n_s[...] = n
    m_s[...] = m

    @pl.when(tb == pl.num_programs(2) - 1)
    def _():
        c1_ref[0, 0] = c
        n1_ref[0, 0] = n
        m1_ref[0, 0] = m


def _mlstm(z, gates, g_head, c0, n0, m0, batch, out_dtype):
    bt = z.shape[0]
    t = bt // batch
    nh, dh = c0.shape[1], c0.shape[2]
    chunk = math.gcd(t, CHUNK_A)
    tblk = _tile(t, (256, 128, 64))
    ntb = t // tblk
    g = gates[:, :2 * nh].reshape(batch, t, 2, nh)
    g_col = jnp.transpose(g, (0, 3, 1, 2))
    g_row = jnp.transpose(g, (0, 3, 2, 1))

    def col(off):
        return pl.BlockSpec((tblk, dh), lambda b, h, i: (b * ntb + i, off + h))

    state = lambda r, c: pl.BlockSpec((1, 1, r, c), lambda b, h, i: (b, h, 0, 0))
    h, c1, n1, m1 = pl.pallas_call(
        functools.partial(_mlstm_kernel, chunk=chunk, n_chunks=tblk // chunk, scale=dh ** -0.5),
        grid=(batch, nh, ntb),
        in_specs=[col(0), col(nh), col(2 * nh), col(3 * nh),
                  pl.BlockSpec((1, 1, tblk, 2), lambda b, h, i: (b, h, i, 0)),
                  pl.BlockSpec((1, 1, 2, tblk), lambda b, h, i: (b, h, 0, i)),
                  pl.BlockSpec((1, 1, dh), lambda b, h, i: (h, 0, 0)),
                  state(dh, dh), state(1, dh), state(1, 1)],
        out_specs=[col(0), state(dh, dh), state(1, dh), state(1, 1)],
        out_shape=[jax.ShapeDtypeStruct((bt, nh * dh), out_dtype),
                   jax.ShapeDtypeStruct((batch, nh, dh, dh), F32),
                   jax.ShapeDtypeStruct((batch, nh, 1, dh), F32),
                   jax.ShapeDtypeStruct((batch, nh, 1, 1), F32)],
        scratch_shapes=[pltpu.VMEM((dh, dh), F32), pltpu.VMEM((1, dh), F32), pltpu.VMEM((1, 1), F32)],
        compiler_params=_params(3),
        name="mlstm",
    )(z, z, z, z, g_col, g_row, g_head.reshape(nh, 1, dh), c0,
      n0.reshape(batch, nh, 1, dh), m0.reshape(batch, nh, 1, 1))
    return h, c1, n1.reshape(batch, nh, dh), m1.reshape(batch, nh)


def _rwkv_prep_kernel(z_ref, zp_ref, zs_ref, zsp_ref, sh_ref, shs_ref, mu_ref, mus_ref,
                      w0_ref, w2_ref, a0_ref, a2_ref, g2_ref, kks_ref, ka_ref,
                      r_ref, ld_ref, k_ref, v_ref, kk_ref, a_ref, g_ref, *, blocks_per_seq, wb, lw, la):
    i = pl.program_id(0)
    first = (i % blocks_per_seq) == 0

    def shifted(cur, prev8, start_row, mu):
        rows = cur.shape[0]
        prev_row = jnp.where(first, start_row, prev8[7:8, :])
        rolled = pltpu.roll(cur, 1, 0) if rows > 1 else cur
        ridx = lax.broadcasted_iota(jnp.int32, cur.shape, 0)
        prev = jnp.where(ridx == 0, prev_row, rolled)
        return cur + mu * (prev - cur)

    zm = shifted(z_ref[...], zp_ref[...], sh_ref[0], mu_ref[...])
    sm = shifted(zs_ref[...], zsp_ref[...], shs_ref[0], mus_ref[...])
    r, k, v = zm[:, :wb], zm[:, wb:2 * wb], zm[:, 2 * wb:]
    wl = sm[:, LANES:2 * LANES]
    al = sm[:, 2 * LANES:3 * LANES]
    gl = sm[:, 3 * LANES:]
    w_pre = w0_ref[...] + _dot(jnp.tanh(wl).astype(BF16), w2_ref[...])
    ld_ref[...] = -jnp.exp(-_softplus(-w_pre) - 0.5)
    a = jax.nn.sigmoid(a0_ref[...] + _dot(al.astype(BF16), a2_ref[...]))
    g_ref[...] = _dot(jax.nn.sigmoid(gl).astype(BF16), g2_ref[...])
    r_ref[...] = r
    v_ref[...] = v
    a_ref[...] = a
    kk_ref[...] = k * kks_ref[...]
    k_ref[...] = k * (1.0 + (a - 1.0) * ka_ref[...])


def _rwkv_prep(zrkv, zs, shift_rkv, shift_s, mu_rkv, mu_s, w0, w2, a0, a2, g2, kk_scale, k_a, batch):
    bt, w3 = zrkv.shape
    wb = w3 // 3
    t = bt // batch
    ns = zs.shape[1]
    lw, la = w2.shape[0], a2.shape[0]
    tm = _tile(t, (128, 64, 32, 16, 8))
    bps = t // tm
    pad_rows = lambda w: jnp.zeros((LANES, wb), BF16).at[:w.shape[0]].set(w.astype(BF16))
    row = lambda n: pl.BlockSpec((tm, n), lambda i: (i, 0))
    prev = lambda n: pl.BlockSpec((8, n), lambda i: (jnp.maximum(i * (tm // 8) - 1, 0), 0))
    seq = lambda n: pl.BlockSpec((1, 1, n), lambda i: (i // bps, 0, 0))
    vec = lambda n: pl.BlockSpec((1, n), lambda i: (0, 0))
    mat = lambda r, c: pl.BlockSpec((r, c), lambda i: (0, 0))
    outs = pl.pallas_call(
        functools.partial(_rwkv_prep_kernel, blocks_per_seq=bps, wb=wb, lw=lw, la=la),
        grid=(bt // tm,),
        in_specs=[row(w3), prev(w3), row(ns), prev(ns), seq(w3), seq(ns), vec(w3), vec(ns),
                  vec(wb), mat(LANES, wb), vec(wb), mat(LANES, wb), mat(g2.shape[0], wb), vec(wb), vec(wb)],
        out_specs=[row(wb)] * 7,
        out_shape=[jax.ShapeDtypeStruct((bt, wb), F32)] * 7,
        compiler_params=_params(1),
        name="rwkv_prep",
    )(zrkv, zrkv, zs, zs, shift_rkv.reshape(batch, 1, w3), shift_s.reshape(batch, 1, ns),
      mu_rkv.reshape(1, w3), mu_s.reshape(1, ns), w0.reshape(1, wb), pad_rows(w2), a0.reshape(1, wb),
      pad_rows(a2), g2.astype(BF16), kk_scale.reshape(1, wb), k_a.reshape(1, wb))
    return outs


def _rwkv_kernel(r_ref, ld_ref, k_ref, v_ref, kk_ref, a_ref, g_ref, rk_ref, gng_ref, gnb_ref, s0_ref,
                 y_ref, s1_ref, s_s, *, chunk, n_chunks, hs):
    tb = pl.program_id(2)

    @pl.when(tb == 0)
    def _():
        s_s[...] = s0_ref[0, 0]

    C = chunk
    R = 2 * C
    W = 2 * hs
    lane = lax.broadcasted_iota(jnp.int32, (R, W), 1)
    rowi = lax.broadcasted_iota(jnp.int32, (R, W), 0)
    headmask = _div_pow2(rowi, C) == _div_pow2(lane, hs)
    ri = lax.broadcasted_iota(jnp.int32, (R, R), 0)
    ci_ = lax.broadcasted_iota(jnp.int32, (R, R), 1)
    same = _div_pow2(ri, C) == _div_pow2(ci_, C)
    lower_incl = same & (ci_ <= ri)
    lower_strict = same & (ci_ < ri)
    ltri = lower_incl.astype(F32)
    eye_r = (ri == ci_).astype(F32)
    wi = lax.broadcasted_iota(jnp.int32, (W, W), 0)
    wj = lax.broadcasted_iota(jnp.int32, (W, W), 1)
    eye_w = (wi == wj).astype(F32)
    lane_row = lax.broadcasted_iota(jnp.int32, (1, W), 1)
    rk = rk_ref[...]
    gng = gng_ref[...]
    gnb = gnb_ref[...]

    def stack(x):
        return jnp.where(headmask, jnp.concatenate([x, x], axis=0), 0.0)

    bf = lambda x: x.astype(BF16)
    pre = []
    for ci in range(n_chunks):
        sl = slice(ci * C, (ci + 1) * C)
        rs, lds, ks, vs = stack(r_ref[sl, :]), stack(ld_ref[sl, :]), stack(k_ref[sl, :]), stack(v_ref[sl, :])
        kks, as_ = stack(kk_ref[sl, :]), stack(a_ref[sl, :])
        lam = jnp.dot(ltri, lds, precision=HIGHEST, preferred_element_type=F32)
        e_pos = jnp.exp(lam)
        e_neg = jnp.exp(-lam)
        kkn = kks * lax.rsqrt(jnp.maximum(jnp.sum(kks * kks, axis=-1, keepdims=True), 1e-24))
        kap = bf(kkn * jnp.exp(lam - lds))
        bt = bf(kkn * as_ * e_neg)
        kt = bf(ks * e_neg)
        rho = rs * e_pos
        vb = bf(vs)
        a_kb = jnp.where(lower_strict, _dot_nt(kap, bt), 0.0)
        a_kk = jnp.where(lower_strict, _dot_nt(kap, kt), 0.0)
        a_rb = bf(jnp.where(lower_incl, _dot_nt(bf(rho), bt), 0.0))
        a_rk = bf(jnp.where(lower_incl, _dot_nt(bf(rho), kt), 0.0))
        tinv = eye_r - jnp.where(_div_pow2(ri, 2) == _div_pow2(ci_, 2), a_kb, 0.0)
        bs = 4
        while bs <= C:
            off = jnp.where((_div_pow2(ri, bs) == _div_pow2(ci_, bs)) & (_mod_pow2(ri, bs) >= bs // 2)
                            & (_mod_pow2(ci_, bs) < bs // 2), a_kb, 0.0)
            tinv = tinv - _dot(bf(_dot(bf(tinv), bf(off))), bf(tinv))
            bs *= 2
        tb16 = bf(tinv)
        av = _dot(bf(a_kk), vb)
        kap2 = bf(_dot(tb16, kap))
        u0 = bf(_dot(tb16, bf(av)))
        gamma = jnp.where(lane_row < hs, e_pos[C - 1:C, :], e_pos[R - 1:R, :])
        m_p = (eye_w - _dot_tn(kap2, bt)) * gamma
        g_p = (_dot_tn(vb, kt) - _dot_tn(u0, bt)) * gamma
        rho2 = rho - _dot(a_rb, kap2)
        y0 = _dot(a_rk, vb) - _dot(a_rb, u0)
        bonus = jnp.sum(rs * ks * rk, axis=-1, keepdims=True)
        pre.append((m_p, g_p, rho2, y0, bonus * vs))

    s = s_s[...]
    for ci in range(n_chunks):
        sl = slice(ci * C, (ci + 1) * C)
        m_p, g_p, rho2, y0, bonus_v = pre[ci]
        ys = lax.dot_general(rho2, s, (((1,), (1,)), ((), ())), precision=HIGHEST, preferred_element_type=F32) + y0
        s = jnp.dot(s, m_p, precision=HIGHEST, preferred_element_type=F32) + g_p
        mean = jnp.sum(ys, axis=-1, keepdims=True) / hs
        d = jnp.where(headmask, ys - mean, 0.0)
        var = jnp.sum(d * d, axis=-1, keepdims=True) / hs
        yn = jnp.where(headmask, d * lax.rsqrt(var + GN_EPS_B) * gng + gnb, 0.0) + bonus_v
        y_ref[sl, :] = ((yn[:C] + yn[C:]) * g_ref[sl, :]).astype(y_ref.dtype)
    s_s[...] = s

    @pl.when(tb == pl.num_programs(2) - 1)
    def _():
        s1_ref[0, 0] = s


def _rwkv(r, ld, k, v, kk, a, g, r_k, gn_g, gn_b, s0, batch, out_dtype):
    bt, wb = r.shape
    t = bt // batch
    nh, hs = s0.shape[1], s0.shape[2]
    assert 2 * hs == LANES and nh % 2 == 0
    npair = nh // 2
    chunk = math.gcd(t, CHUNK_B)
    tblk = _tile(t, (256, 128, 64))
    ntb = t // tblk
    s0p = s0.reshape(batch, npair, 2, hs, hs)
    zero = jnp.zeros_like(s0p[:, :, 0])
    s0bd = jnp.concatenate([jnp.concatenate([s0p[:, :, 0], zero], axis=-1),
                            jnp.concatenate([zero, s0p[:, :, 1]], axis=-1)], axis=-2)
    blk = pl.BlockSpec((tblk, LANES), lambda b, p, i: (b * ntb + i, p))
    vec = pl.BlockSpec((1, LANES), lambda b, p, i: (0, p))
    st = pl.BlockSpec((1, 1, LANES, LANES), lambda b, p, i: (b, p, 0, 0))
    y, s1bd = pl.pallas_call(
        functools.partial(_rwkv_kernel, chunk=chunk, n_chunks=tblk // chunk, hs=hs),
        grid=(batch, npair, ntb),
        in_specs=[blk] * 7 + [vec] * 3 + [st],
        out_specs=[blk, st],
        out_shape=[jax.ShapeDtypeStruct((bt, wb), out_dtype),
                   jax.ShapeDtypeStruct((batch, npair, LANES, LANES), F32)],
        scratch_shapes=[pltpu.VMEM((LANES, LANES), F32)],
        compiler_params=_params(3),
        name="rwkv",
    )(r, ld, k, v, kk, a, g, r_k.reshape(1, wb), gn_g.reshape(1, wb), gn_b.reshape(1, wb), s0bd)
    s1 = jnp.stack([s1bd[:, :, :hs, :hs], s1bd[:, :, hs:, hs:]], axis=2).reshape(batch, nh, hs, hs)
    return y, s1


def _lf_cum_kernel(z_ref, b_ref, lf_ref, g_ref, carry_s):
    @pl.when(pl.program_id(1) == 0)
    def _():
        carry_s[...] = jnp.zeros_like(carry_s)

    lf = _log_sigmoid(z_ref[...] + b_ref[...])
    rows = lf.shape[0]
    ti = lax.broadcasted_iota(jnp.int32, (rows, rows), 0)
    si = lax.broadcasted_iota(jnp.int32, (rows, rows), 1)
    cum = jnp.dot((si <= ti).astype(F32), lf, precision=HIGHEST, preferred_element_type=F32) + carry_s[...]
    lf_ref[...] = lf
    g_ref[...] = cum
    carry_s[...] = cum[rows - 1:rows, :]


def _lf_cum(zf, b_f, batch):
    bt = zf.shape[0]
    t = bt // batch
    tb = _tile(t, (256, 128, 64, 32, 16, 8))
    ntb = t // tb
    bpad = jnp.zeros((1, LANES), F32).at[0, :b_f.shape[0]].set(b_f)
    blk = pl.BlockSpec((tb, LANES), lambda b, i: (b * ntb + i, 0))
    return pl.pallas_call(
        _lf_cum_kernel,
        grid=(batch, ntb),
        in_specs=[blk, pl.BlockSpec((1, LANES), lambda b, i: (0, 0))],
        out_specs=[blk, blk],
        out_shape=[jax.ShapeDtypeStruct((bt, LANES), F32)] * 2,
        scratch_shapes=[pltpu.VMEM((1, LANES), F32)],
        compiler_params=_params(2),
        name="lf_cum",
    )(zf, bpad)


def _fox_kernel(q_ref, k_ref, v_ref, gq_ref, gk_ref, o_ref, kb_s, vb_s, *, bq, scale):
    qi = pl.program_id(2)

    @pl.when(qi == 0)
    def _():
        kb_s[...] = k_ref[...].astype(BF16)
        vb_s[...] = v_ref[...].astype(BF16)

    q = q_ref[...].astype(BF16)
    gq = gq_ref[0, 0]
    d = q.shape[1]
    qpos = qi * bq + lax.broadcasted_iota(jnp.int32, (bq, bq), 0)
    kloc = lax.broadcasted_iota(jnp.int32, (bq, bq), 1)

    def body(kb, carry):
        m, l, acc = carry
        off = pl.multiple_of(kb * bq, bq)
        k = kb_s[pl.ds(off, bq), :]
        v = vb_s[pl.ds(off, bq), :]
        s = _dot_nt(q, k) * scale + gq - gk_ref[0, 0, kb]
        s = jnp.where(kb * bq + kloc <= qpos, s, NEG_INF)
        m_new = jnp.maximum(m, jnp.max(s, axis=-1, keepdims=True))
        p = jnp.exp(s - m_new)
        alpha = jnp.exp(m - m_new)
        l = alpha * l + jnp.sum(p, axis=-1, keepdims=True)
        acc = alpha * acc + _dot(p.astype(BF16), v)
        return m_new, l, acc

    m0 = jnp.full((bq, 1), NEG_INF, F32)
    l0 = jnp.zeros((bq, 1), F32)
    a0 = jnp.zeros((bq, d), F32)
    _, l, acc = lax.fori_loop(0, qi + 1, body, (m0, l0, a0))
    o_ref[...] = (acc / l).astype(o_ref.dtype)


def _fox_prompt(q, k, v, g, batch, nh):
    bt, width = q.shape
    t = bt // batch
    dh = width // nh
    bq = _tile(t, (256, 128))
    nq = t // bq
    gq = jnp.transpose(g, (0, 2, 1)).reshape(batch, nh, t, 1)
    gk = jnp.transpose(g, (0, 2, 1)).reshape(batch, nh, nq, 1, bq)
    return pl.pallas_call(
        functools.partial(_fox_kernel, bq=bq, scale=dh ** -0.5),
        grid=(batch, nh, nq),
        in_specs=[pl.BlockSpec((bq, dh), lambda b, h, i: (b * nq + i, h)),
                  pl.BlockSpec((t, dh), lambda b, h, i: (b, h)),
                  pl.BlockSpec((t, dh), lambda b, h, i: (b, h)),
                  pl.BlockSpec((1, 1, bq, 1), lambda b, h, i: (b, h, i, 0)),
                  pl.BlockSpec((1, 1, nq, 1, bq), lambda b, h, i: (b, h, 0, 0, 0))],
        out_specs=pl.BlockSpec((bq, dh), lambda b, h, i: (b * nq + i, h)),
        out_shape=jax.ShapeDtypeStruct((bt, width), BF16),
        scratch_shapes=[pltpu.VMEM((t, dh), BF16), pltpu.VMEM((t, dh), BF16)],
        compiler_params=_params(3),
        name="fox_prompt",
    )(q, k, v, gq, gk)


def _gpast_kernel(pt_ref, lf_ref, o_ref, carry_s):
    del pt_ref

    @pl.when(pl.program_id(1) == 0)
    def _():
        carry_s[...] = jnp.zeros_like(carry_s)

    lf = lf_ref[0, 0].astype(F32)
    rows = lf.shape[0]
    ti = lax.broadcasted_iota(jnp.int32, (rows, rows), 0)
    si = lax.broadcasted_iota(jnp.int32, (rows, rows), 1)
    later = jnp.dot((si > ti).astype(F32), lf, precision=HIGHEST, preferred_element_type=F32)
    o_ref[0, 0] = -(later + carry_s[...])
    carry_s[...] = carry_s[...] + jnp.sum(lf, axis=0, keepdims=True)


def _gpast(cache_logf, layer, page_table):
    batch, npg = page_table.shape
    _, _, page, nh = cache_logf.shape
    grid_spec = pltpu.PrefetchScalarGridSpec(
        num_scalar_prefetch=1,
        grid=(batch, npg),
        in_specs=[pl.BlockSpec((1, 1, page, nh), lambda b, p, pt: (layer, pt[b, npg - 1 - p], 0, 0))],
        out_specs=pl.BlockSpec((1, 1, page, nh), lambda b, p, pt: (b, npg - 1 - p, 0, 0)),
        scratch_shapes=[pltpu.VMEM((1, nh), F32)],
    )
    return pl.pallas_call(
        _gpast_kernel,
        grid_spec=grid_spec,
        out_shape=jax.ShapeDtypeStruct((batch, npg, page, nh), F32),
        compiler_params=_params(2),
        name="gpast",
    )(page_table, cache_logf)


def _fox_dec_kernel(pt_ref, q_ref, kp_ref, vp_ref, gp_ref, kn_ref, vn_ref, gq_ref, gn_ref, o_ref,
                    m_s, l_s, acc_s, *, nh, scale):
    del pt_ref
    p = pl.program_id(1)

    @pl.when(p == 0)
    def _():
        m_s[...] = jnp.full_like(m_s, NEG_INF)
        l_s[...] = jnp.zeros_like(l_s)
        acc_s[...] = jnp.zeros_like(acc_s)

    q = q_ref[0].astype(BF16)
    gq = gq_ref[0]
    nq = q.shape[0]

    def attend(k, v, gk, causal):
        nk = k.shape[0]
        s = _dot_nt(q, k.astype(BF16)) * scale + gq - gk
        qi = lax.broadcasted_iota(jnp.int32, (nq, nk), 0)
        ki = lax.broadcasted_iota(jnp.int32, (nq, nk), 1)
        keep = _mod_pow2(qi, nh) == _mod_pow2(ki, nh)
        if causal:
            keep = keep & (_div_pow2(ki, nh) <= _div_pow2(qi, nh))
        s = jnp.where(keep, s, NEG_INF)
        m_new = jnp.maximum(m_s[...], jnp.max(s, axis=-1, keepdims=True))
        pr = jnp.exp(s - m_new)
        alpha = jnp.exp(m_s[...] - m_new)
        l_s[...] = alpha * l_s[...] + jnp.sum(pr, axis=-1, keepdims=True)
        acc_s[...] = alpha * acc_s[...] + _dot(pr.astype(BF16), v.astype(BF16))
        m_s[...] = m_new

    page, _, dh = kp_ref.shape[2:]
    attend(kp_ref[0, 0].reshape(page * nh, dh), vp_ref[0, 0].reshape(page * nh, dh), gp_ref[0, 0], False)

    @pl.when(p == pl.num_programs(1) - 1)
    def _():
        attend(kn_ref[0], vn_ref[0], gn_ref[0], True)
        o_ref[0] = (acc_s[...] / l_s[...]).astype(o_ref.dtype)


def _fox_decode(q, kn, vn, g_new, cache_k, cache_v, g_past, layer, page_table, nh):
    batch, rows, dh = q.shape
    npg = page_table.shape[1]
    page = cache_k.shape[2]
    gq = g_new.reshape(batch, rows, 1)
    gn = g_new.reshape(batch, 1, rows)
    gp = g_past.reshape(batch, npg, 1, page * nh)
    new = lambda n: pl.BlockSpec((1, rows, n), lambda b, p, pt: (b, 0, 0))
    pg = pl.BlockSpec((1, 1, page, nh, dh), lambda b, p, pt: (layer, pt[b, p], 0, 0, 0))
    grid_spec = pltpu.PrefetchScalarGridSpec(
        num_scalar_prefetch=1,
        grid=(batch, npg),
        in_specs=[new(dh), pg, pg,
                  pl.BlockSpec((1, 1, 1, page * nh), lambda b, p, pt: (b, p, 0, 0)),
                  new(dh), new(dh), new(1),
                  pl.BlockSpec((1, 1, rows), lambda b, p, pt: (b, 0, 0))],
        out_specs=new(dh),
        scratch_shapes=[pltpu.VMEM((rows, 1), F32), pltpu.VMEM((rows, 1), F32), pltpu.VMEM((rows, dh), F32)],
    )
    return pl.pallas_call(
        functools.partial(_fox_dec_kernel, nh=nh, scale=dh ** -0.5),
        grid_spec=grid_spec,
        out_shape=jax.ShapeDtypeStruct((batch, rows, dh), F32),
        compiler_params=_params(2),
        name="fox_decode",
    )(page_table, q, cache_k, cache_v, gp, kn, vn, gq, gn)


def _even_layer(xb, batch, st, w, big):
    c0, n0, m0, s0, shift0 = st
    nh_a, dh_a = c0.shape[1], c0.shape[2]
    wa = nh_a * dh_a
    bt = xb.shape[0]
    t = bt // batch
    act = BF16 if big else F32
    z_a = _mm(xb, w["w_qkvo"], act)
    zs = _mm(xb, w["w_small"], F32)
    zrkv = _mm(xb, w["w_rkv"], F32)
    gates = _mlstm_gates(zs, w["b_gate"], nh_a)
    h, c1, n1, m1 = _mlstm(z_a, gates, w["g_head"], c0, n0, m0, batch, act)
    wb3 = zrkv.shape[1]
    lw, la, lg = w["lw"], w["la"], w["lg"]
    sh_rkv = shift0[:, :wb3]
    sh_s = jnp.zeros((batch, 3 * LANES + lg), F32)
    sh_s = sh_s.at[:, LANES:LANES + lw].set(shift0[:, wb3:wb3 + lw])
    sh_s = sh_s.at[:, 2 * LANES:2 * LANES + la].set(shift0[:, wb3 + lw:wb3 + lw + la])
    sh_s = sh_s.at[:, 3 * LANES:].set(shift0[:, wb3 + lw + la:])
    r, ld, k, v, kk, a, g = _rwkv_prep(zrkv, zs, sh_rkv, sh_s, w["mu_rkv"], w["mu_s"], w["w0"], w["w2"],
                                       w["a0"], w["a2"], w["g2"], w["kk_scale"], w["k_a"], batch)
    y, s1 = _rwkv(r, ld, k, v, kk, a, g, w["r_k"], w["gn_g"], w["gn_b"], s0, batch, act)
    mix = _mm2(h.astype(BF16), y.astype(BF16), w["w_out"])
    last_rkv = zrkv.reshape(batch, t, wb3)[:, -1]
    last_s = zs.reshape(batch, t, -1)[:, -1]
    shift1 = jnp.concatenate([last_rkv, last_s[:, LANES:LANES + lw], last_s[:, 2 * LANES:2 * LANES + la],
                              last_s[:, 3 * LANES:]], axis=-1)
    return mix, (c1, n1, m1, s1, shift1)


def _fox_project(xb, w, big):
    q = _mm(xb, w["w_q"], BF16 if big else F32)
    k = _mm(xb, w["w_k"], F32)
    v = _mm(xb, w["w_v"], F32)
    zf = _mm(xb, w["w_f"], F32)
    return q, k, v, zf


def _post_block(x, mix, pb, w, alpha):
    x1, x1b = _add_ln(x, mix, w["ln_mix_g"], w["ln_mix_b"], alpha)
    f = _ffn(x1b, w["w_ffn_in"], w["w_ffn_out"])
    x2, x2b = _add_ln(x1, f, w["ln_ffn_g"], w["ln_ffn_b"], alpha)
    return _ple(x2, x2b, pb, w["w_ple_gate"], w["w_ple"])


def kernel(x_prompt, x_sample, cache_k, cache_v, cache_logf, state_mlstm_c, state_mlstm_n, state_mlstm_m, state_rwkv_wkv, state_rwkv_shift, page_table, p_prompt, p_sample, w_in_ab, b_gate_a, g_head_a, mu_b, w0_b, w2_b, a0_b, a2_b, g2_b, kk_scale_b, ka_b, rk_b, gn_g_b, gn_b_b, w_out_ab, w_in_c, b_f_c, w_out_c, ln_mix_g, ln_mix_b, ln_ffn_g, ln_ffn_b, w_ffn_in, w_ffn_out, w_ple, w_ple_gate):
    bp, tp, d = x_prompt.shape
    bs, ts, _ = x_sample.shape
    depth = ln_mix_g.shape[0]
    alpha = (2 * depth) ** 0.25
    nh_a, dh_a = state_mlstm_c.shape[2], state_mlstm_c.shape[3]
    wa = nh_a * dh_a
    nh_b, hs_b = state_rwkv_wkv.shape[2], state_rwkv_wkv.shape[3]
    wb = nh_b * hs_b
    lw, la, lg = w2_b.shape[1], a2_b.shape[1], g2_b.shape[1]
    nh_c, dh_c = cache_k.shape[3], cache_k.shape[4]
    n_in_a = 4 * wa + 2 * nh_a
    assert lw <= LANES and la <= LANES and 2 * nh_a <= LANES and nh_c <= LANES

    xp, xs = x_prompt.reshape(bp * tp, d), x_sample.reshape(bs * ts, d)
    xpb, xsb = xp.astype(BF16), xs.astype(BF16)
    pp = p_prompt.reshape(depth, bp * tp, -1).astype(BF16)
    ps = p_sample.reshape(depth, bs * ts, -1).astype(BF16)

    outs = {name: [] for name in ("kp", "vp", "lfp", "ks", "vs", "lfs", "cp", "np", "mp", "cs", "ns", "ms",
                                  "sp", "shp", "ss", "shs")}
    for i in range(depth):
        j = i // 2
        w = {"ln_mix_g": ln_mix_g[i], "ln_mix_b": ln_mix_b[i], "ln_ffn_g": ln_ffn_g[i], "ln_ffn_b": ln_ffn_b[i],
             "w_ffn_in": w_ffn_in[i].astype(BF16), "w_ffn_out": w_ffn_out[i].astype(BF16),
             "w_ple": w_ple[i].astype(BF16), "w_ple_gate": w_ple_gate[i].astype(BF16)}
        if i % 2 == 0:
            w_in = w_in_ab[j]
            rk0 = n_in_a
            lo = rk0 + 3 * wb
            zc = lambda n: jnp.zeros((d, n), F32)
            w_small = jnp.concatenate([w_in[:, 4 * wa:n_in_a], zc(LANES - 2 * nh_a),
                                       w_in[:, lo:lo + lw], zc(LANES - lw),
                                       w_in[:, lo + lw:lo + lw + la], zc(LANES - la),
                                       w_in[:, lo + lw + la:]], axis=1)
            mu = mu_b[j]
            zv = lambda n: jnp.zeros((n,), F32)
            mu_s = jnp.concatenate([zv(LANES), mu[3 * wb:3 * wb + lw], zv(LANES - lw),
                                    mu[3 * wb + lw:3 * wb + lw + la], zv(LANES - la), mu[3 * wb + lw + la:]])
            w.update({"w_qkvo": w_in[:, :4 * wa].astype(BF16), "w_small": w_small.astype(BF16),
                      "w_rkv": w_in[:, rk0:lo].astype(BF16), "b_gate": b_gate_a[j], "g_head": g_head_a[j],
                      "mu_rkv": mu[:3 * wb], "mu_s": mu_s, "w0": w0_b[j], "w2": w2_b[j], "a0": a0_b[j],
                      "a2": a2_b[j], "g2": g2_b[j], "kk_scale": kk_scale_b[j], "k_a": ka_b[j],
                      "r_k": rk_b[j].reshape(-1), "gn_g": gn_g_b[j], "gn_b": gn_b_b[j],
                      "w_out": w_out_ab[j].astype(BF16), "lw": lw, "la": la, "lg": lg})
            zero_state = (jnp.zeros((bp, nh_a, dh_a, dh_a), F32), jnp.zeros((bp, nh_a, dh_a), F32),
                          jnp.zeros((bp, nh_a), F32), jnp.zeros((bp, nh_b, hs_b, hs_b), F32),
                          jnp.zeros((bp, mu.shape[0]), F32))
            mix_p, (cp, np_, mp, sp, shp) = _even_layer(xpb, bp, zero_state, w, True)
            mix_s, (cs, ns, ms, ss, shs) = _even_layer(
                xsb, bs, (state_mlstm_c[j], state_mlstm_n[j], state_mlstm_m[j], state_rwkv_wkv[j],
                          state_rwkv_shift[j]), w, False)
            for name, val in (("cp", cp), ("np", np_), ("mp", mp), ("sp", sp), ("shp", shp),
                              ("cs", cs), ("ns", ns), ("ms", ms), ("ss", ss), ("shs", shs)):
                outs[name].append(val)
        else:
            w_in = w_in_c[j]
            w_f = jnp.concatenate([w_in[:, 3 * d:], jnp.zeros((d, LANES - nh_c), F32)], axis=1)
            w.update({"w_q": w_in[:, :d].astype(BF16), "w_k": w_in[:, d:2 * d].astype(BF16),
                      "w_v": w_in[:, 2 * d:3 * d].astype(BF16), "w_f": w_f.astype(BF16)})
            w_out = w_out_c[j].astype(BF16)
            qp, kp, vp, zfp = _fox_project(xpb, w, True)
            lfp, gp = _lf_cum(zfp, b_f_c[j], bp)
            lfp = lfp[:, :nh_c].reshape(bp, tp, nh_c)
            att_p = _fox_prompt(qp, kp, vp, gp[:, :nh_c].reshape(bp, tp, nh_c), bp, nh_c)
            mix_p = _mm(att_p, w_out, F32)
            qs, kn, vn, zfs = _fox_project(xsb, w, False)
            lfs, g_new = _lf_cum(zfs, b_f_c[j], bs)
            lfs = lfs[:, :nh_c].reshape(bs, ts, nh_c)
            g_past = _gpast(cache_logf, j, page_table)
            rows = ts * nh_c
            att_s = _fox_decode(qs.reshape(bs, rows, dh_c), kn.reshape(bs, rows, dh_c), vn.reshape(bs, rows, dh_c),
                                g_new[:, :nh_c].reshape(bs, ts, nh_c), cache_k, cache_v, g_past, j, page_table, nh_c)
            mix_s = _mm(att_s.reshape(bs * ts, d).astype(BF16), w_out, F32)
            for name, val in (("kp", kp.reshape(bp, tp, nh_c, dh_c)), ("vp", vp.reshape(bp, tp, nh_c, dh_c)),
                              ("lfp", lfp), ("ks", kn.reshape(bs, ts, nh_c, dh_c)),
                              ("vs", vn.reshape(bs, ts, nh_c, dh_c)), ("lfs", lfs)):
                outs[name].append(val)
        xp, xpb = _post_block(xp, mix_p, pp[i], w, alpha)
        xs, xsb = _post_block(xs, mix_s, ps[i], w, alpha)
    stk = lambda name: jnp.stack(outs[name])
    return (xp.reshape(bp, tp, d), xs.reshape(bs, ts, d),
            stk("kp"), stk("vp"), stk("lfp"), stk("ks"), stk("vs"), stk("lfs"),
            stk("cp"), stk("np"), stk("mp"), stk("cs"), stk("ns"), stk("ms"),
            stk("sp"), stk("shp"), stk("ss"), stk("shs"))
```

```python
import functools
import math

import jax
import jax.numpy as jnp
from jax import lax
from jax.experimental import pallas as pl
from jax.experimental.pallas import tpu as pltpu

F32 = jnp.float32
BF16 = jnp.bfloat16
HIGHEST = lax.Precision.HIGHEST

LANES = 128
VMEM_LIMIT_BYTES = 56 * 1024 * 1024
PANEL_DOUBLE_BUFFER_BYTES = VMEM_LIMIT_BYTES // 4

CHUNK_A = 64
GATE_CAP = 15.0
HEAD_EPS_A = 1e-6
CHUNK_B = 64
GN_EPS_B = 64e-5
LN_EPS = 1e-5
NEG_INF = float("-inf")


def _tile(dim, prefs):
    for t in prefs:
        if t <= dim and dim % t == 0:
            return t
    return dim


def _params(n_axes):
    return pltpu.CompilerParams(dimension_semantics=("arbitrary",) * n_axes,
                                vmem_limit_bytes=VMEM_LIMIT_BYTES)


def _dot(a, b):
    return jnp.dot(a, b, preferred_element_type=F32)


def _dot_nt(a, b):
    return lax.dot_general(a, b, (((1,), (1,)), ((), ())), preferred_element_type=F32)


def _dot_tn(a, b):
    return lax.dot_general(a, b, (((0,), (0,)), ((), ())), preferred_element_type=F32)


def _div_pow2(x, n):
    assert n & (n - 1) == 0
    return lax.shift_right_logical(x, n.bit_length() - 1)


def _mod_pow2(x, n):
    assert n & (n - 1) == 0
    return jnp.bitwise_and(x, n - 1)


def _log_sigmoid(x):
    return jnp.minimum(x, 0.0) - jnp.log1p(jnp.exp(-jnp.abs(x)))


def _softplus(x):
    return jnp.maximum(x, 0.0) + jnp.log1p(jnp.exp(-jnp.abs(x)))


def _mm_kernel(x_ref, w_ref, o_ref):
    o_ref[...] = _dot(x_ref[...], w_ref[...]).astype(o_ref.dtype)


def _mm(x, w, out_dtype):
    m, k = x.shape
    n = w.shape[1]
    tm = _tile(m, (1024, 512, 256, 128, 64, 32, 16, 8))
    panel_bytes = tm * k * x.dtype.itemsize
    big_panel = panel_bytes > PANEL_DOUBLE_BUFFER_BYTES
    tn = _tile(n, (256, 128) if big_panel else (512, 384, 256, 128))
    x_spec = pl.BlockSpec((tm, k), lambda i, j: (i, 0), **({"pipeline_mode": pl.Buffered(1)} if big_panel else {}))
    return pl.pallas_call(
        _mm_kernel,
        grid=(m // tm, n // tn),
        in_specs=[x_spec,
                  pl.BlockSpec((k, tn), lambda i, j: (0, j))],
        out_specs=pl.BlockSpec((tm, tn), lambda i, j: (i, j)),
        out_shape=jax.ShapeDtypeStruct((m, n), out_dtype),
        compiler_params=_params(2),
        name="mm",
    )(x, w)


def _mm2_kernel(x1_ref, x2_ref, w1_ref, w2_ref, o_ref):
    o_ref[...] = _dot(x1_ref[...], w1_ref[...]) + _dot(x2_ref[...], w2_ref[...])


def _mm2(x1, x2, w):
    m, k = x1.shape
    assert x2.shape == (m, k) and w.shape[0] == 2 * k
    n = w.shape[1]
    tm = _tile(m, (1024, 512, 256, 128, 64, 32, 16, 8))
    tn = _tile(n, (512, 384, 256, 128))
    return pl.pallas_call(
        _mm2_kernel,
        grid=(m // tm, n // tn),
        in_specs=[pl.BlockSpec((tm, k), lambda i, j: (i, 0)),
                  pl.BlockSpec((tm, k), lambda i, j: (i, 0)),
                  pl.BlockSpec((k, tn), lambda i, j: (0, j)),
                  pl.BlockSpec((k, tn), lambda i, j: (1, j))],
        out_specs=pl.BlockSpec((tm, tn), lambda i, j: (i, j)),
        out_shape=jax.ShapeDtypeStruct((m, n), F32),
        compiler_params=_params(2),
        name="mm2",
    )(x1, x2, w, w)


def _add_ln_kernel(x_ref, y_ref, g_ref, b_ref, o_ref, ob_ref, *, alpha):
    v = alpha * x_ref[...] + y_ref[...]
    mu = jnp.mean(v, axis=-1, keepdims=True)
    d = v - mu
    var = jnp.mean(d * d, axis=-1, keepdims=True)
    out = d * lax.rsqrt(var + LN_EPS) * g_ref[...] + b_ref[...]
    o_ref[...] = out
    ob_ref[...] = out.astype(BF16)


def _add_ln(x, y, g, b, alpha):
    m, d = x.shape
    tm = _tile(m, (256, 128, 64, 32, 16))
    row = pl.BlockSpec((tm, d), lambda i: (i, 0))
    vec = pl.BlockSpec((1, d), lambda i: (0, 0))
    return pl.pallas_call(
        functools.partial(_add_ln_kernel, alpha=alpha),
        grid=(m // tm,),
        in_specs=[row, row, vec, vec],
        out_specs=[row, row],
        out_shape=[jax.ShapeDtypeStruct((m, d), F32), jax.ShapeDtypeStruct((m, d), BF16)],
        compiler_params=_params(1),
        name="add_ln",
    )(x, y, g.reshape(1, d), b.reshape(1, d))


def _swiglu_kernel(x_ref, wg_ref, wu_ref, o_ref):
    x = x_ref[...]
    gate = _dot(x, wg_ref[...])
    o_ref[...] = (gate * jax.nn.sigmoid(gate) * _dot(x, wu_ref[...])).astype(o_ref.dtype)


def _swiglu(xb, w_in):
    m, d = xb.shape
    f = w_in.shape[1] // 2
    tm = _tile(m, (1024, 512, 256, 128, 64, 32, 16))
    tf = _tile(f, (256, 128))
    nf = f // tf
    return pl.pallas_call(
        _swiglu_kernel,
        grid=(m // tm, nf),
        in_specs=[pl.BlockSpec((tm, d), lambda i, j: (i, 0)),
                  pl.BlockSpec((d, tf), lambda i, j: (0, j)),
                  pl.BlockSpec((d, tf), lambda i, j: (0, j + nf))],
        out_specs=pl.BlockSpec((tm, tf), lambda i, j: (i, j)),
        out_shape=jax.ShapeDtypeStruct((m, f), BF16),
        compiler_params=_params(2),
        name="swiglu",
    )(xb, w_in, w_in)


def _ffn(xb, w_in, w_out):
    return _mm(_swiglu(xb, w_in), w_out, F32)


def _ple_kernel(xb_ref, wg_ref, pb_ref, wp_ref, x_ref, o_ref, ob_ref):
    gate = jax.nn.sigmoid(_dot(xb_ref[...], wg_ref[...]))
    out = x_ref[...] + gate * _dot(pb_ref[...], wp_ref[...])
    o_ref[...] = out
    ob_ref[...] = out.astype(BF16)


def _ple(x, xb, pb, w_gate, w_ple):
    m, d = x.shape
    dp = pb.shape[1]
    tm = _tile(m, (1024, 512, 256, 128, 64, 32, 16))
    tn = _tile(d, (512, 256, 128))
    tile = pl.BlockSpec((tm, tn), lambda i, j: (i, j))
    return pl.pallas_call(
        _ple_kernel,
        grid=(m // tm, d // tn),
        in_specs=[pl.BlockSpec((tm, d), lambda i, j: (i, 0)),
                  pl.BlockSpec((d, tn), lambda i, j: (0, j)),
                  pl.BlockSpec((tm, dp), lambda i, j: (i, 0)),
                  pl.BlockSpec((dp, tn), lambda i, j: (0, j)),
                  tile],
        out_specs=[tile, tile],
        out_shape=[jax.ShapeDtypeStruct((m, d), F32), jax.ShapeDtypeStruct((m, d), BF16)],
        compiler_params=_params(2),
        name="ple",
    )(xb, w_gate, pb, w_ple, x)


def _mlstm_gate_kernel(z_ref, b_ref, o_ref, *, nh):
    g = GATE_CAP * jnp.tanh((z_ref[...] + b_ref[...]) / GATE_CAP)
    lane = lax.broadcasted_iota(jnp.int32, g.shape, 1)
    o_ref[...] = jnp.where(lane < nh, g, _log_sigmoid(g))


def _mlstm_gates(zs, b_gate, nh):
    m = zs.shape[0]
    tm = _tile(m, (1024, 512, 256, 128, 64, 32, 16, 8))
    bpad = jnp.zeros((1, LANES), F32).at[0, :2 * nh].set(b_gate)
    return pl.pallas_call(
        functools.partial(_mlstm_gate_kernel, nh=nh),
        grid=(m // tm,),
        in_specs=[pl.BlockSpec((tm, LANES), lambda i: (i, 0)),
                  pl.BlockSpec((1, LANES), lambda i: (0, 0))],
        out_specs=pl.BlockSpec((tm, LANES), lambda i: (i, 0)),
        out_shape=jax.ShapeDtypeStruct((m, LANES), F32),
        compiler_params=_params(1),
        name="mlstm_gates",
    )(zs, bpad)


def _mlstm_kernel(q_ref, k_ref, v_ref, o_ref, gc_ref, gr_ref, gh_ref, c0_ref, n0_ref, m0_ref,
                  h_ref, c1_ref, n1_ref, m1_ref, c_s, n_s, m_s, *, chunk, n_chunks, scale):
    tb = pl.program_id(2)

    @pl.when(tb == 0)
    def _():
        c_s[...] = c0_ref[0, 0]
        n_s[...] = n0_ref[0, 0]
        m_s[...] = m0_ref[0, 0]

    L = chunk
    ti = lax.broadcasted_iota(jnp.int32, (L, L), 0)
    si = lax.broadcasted_iota(jnp.int32, (L, L), 1)
    causal = si <= ti
    gc = gc_ref[0, 0]
    gr = gr_ref[0, 0]
    c = c_s[...]
    n = n_s[...]
    m = m_s[...]
    for ci in range(n_chunks):
        sl = slice(ci * L, (ci + 1) * L)
        q = q_ref[sl, :].astype(BF16)
        ks = k_ref[sl, :].astype(F32) * scale
        v = v_ref[sl, :].astype(BF16)
        ig_col, lf_col = gc[sl, 0:1], gc[sl, 1:2]
        ig_row, lf_row = gr[0:1, sl], gr[1:2, sl]
        bcum_col = jnp.sum(jnp.where(causal, lf_row, 0.0), axis=1, keepdims=True)
        bcum_row = jnp.sum(jnp.where(ti <= si, lf_col, 0.0), axis=0, keepdims=True)
        inter = bcum_col + m
        logd = jnp.where(causal, bcum_col - bcum_row + ig_row, NEG_INF)
        m_t = jnp.maximum(inter, jnp.max(logd, axis=1, keepdims=True))
        w_carry = jnp.exp(inter - m_t)
        s = _dot_nt(q, ks.astype(BF16)) * jnp.exp(logd - m_t)
        num = _dot(s.astype(BF16), v) + w_carry * _dot(q, c.astype(BF16))
        qn = jnp.sum(q.astype(F32) * n, axis=1, keepdims=True)
        den = jnp.sum(s, axis=1, keepdims=True) + w_carry * qn
        h = num / jnp.maximum(jnp.abs(den), jnp.exp(-m_t))
        m_new = m_t[L - 1:L, :]
        w_c = jnp.exp(inter[L - 1:L, :] - m_new)
        w_rows = jnp.exp(bcum_col[L - 1:L, :] - bcum_col + ig_col - m_new)
        kw = w_rows * ks
        c = w_c * c + _dot_tn(kw.astype(BF16), v)
        n = w_c * n + jnp.sum(kw, axis=0, keepdims=True)
        m = m_new
        mu = jnp.mean(h, axis=-1, keepdims=True)
        d = h - mu
        var = jnp.mean(d * d, axis=-1, keepdims=True)
        hn = d * lax.rsqrt(var + HEAD_EPS_A) * gh_ref[0] * jax.nn.sigmoid(o_ref[sl, :].astype(F32))
        h_ref[sl, :] = hn.astype(h_ref.dtype)
    c_s[...] = c
    n_s[...] = n
    m_s[...] = m

    @pl.when(tb == pl.num_programs(2) - 1)
    def _():
        c1_ref[0, 0] = c
        n1_ref[0, 0] = n
        m1_ref[0, 0] = m


def _mlstm(z, gates, g_head, c0, n0, m0, batch, out_dtype):
    bt = z.shape[0]
    t = bt // batch
    nh, dh = c0.shape[1], c0.shape[2]
    chunk = math.gcd(t, CHUNK_A)
    tblk = _tile(t, (256, 128, 64))
    ntb = t // tblk
    g = gates[:, :2 * nh].reshape(batch, t, 2, nh)
    g_col = jnp.transpose(g, (0, 3, 1, 2))
    g_row = jnp.transpose(g, (0, 3, 2, 1))

    def col(off):
        return pl.BlockSpec((tblk, dh), lambda b, h, i: (b * ntb + i, off + h))

    state = lambda r, c: pl.BlockSpec((1, 1, r, c), lambda b, h, i: (b, h, 0, 0))
    h, c1, n1, m1 = pl.pallas_call(
        functools.partial(_mlstm_kernel, chunk=chunk, n_chunks=tblk // chunk, scale=dh ** -0.5),
        grid=(batch, nh, ntb),
        in_specs=[col(0), col(nh), col(2 * nh), col(3 * nh),
                  pl.BlockSpec((1, 1, tblk, 2), lambda b, h, i: (b, h, i, 0)),
                  pl.BlockSpec((1, 1, 2, tblk), lambda b, h, i: (b, h, 0, i)),
                  pl.BlockSpec((1, 1, dh), lambda b, h, i: (h, 0, 0)),
                  state(dh, dh), state(1, dh), state(1, 1)],
        out_specs=[col(0), state(dh, dh), state(1, dh), state(1, 1)],
        out_shape=[jax.ShapeDtypeStruct((bt, nh * dh), out_dtype),
                   jax.ShapeDtypeStruct((batch, nh, dh, dh), F32),
                   jax.ShapeDtypeStruct((batch, nh, 1, dh), F32),
                   jax.ShapeDtypeStruct((batch, nh, 1, 1), F32)],
        scratch_shapes=[pltpu.VMEM((dh, dh), F32), pltpu.VMEM((1, dh), F32), pltpu.VMEM((1, 1), F32)],
        compiler_params=_params(3),
        name="mlstm",
    )(z, z, z, z, g_col, g_row, g_head.reshape(nh, 1, dh), c0,
      n0.reshape(batch, nh, 1, dh), m0.reshape(batch, nh, 1, 1))
    return h, c1, n1.reshape(batch, nh, dh), m1.reshape(batch, nh)


def _rwkv_prep_kernel(z_ref, zp_ref, zs_ref, zsp_ref, sh_ref, shs_ref, mu_ref, mus_ref,
                      w0_ref, w2_ref, a0_ref, a2_ref, g2_ref, kks_ref, ka_ref,
                      r_ref, ld_ref, k_ref, v_ref, kk_ref, a_ref, g_ref, *, blocks_per_seq, wb, lw, la):
    i = pl.program_id(0)
    first = (i % blocks_per_seq) == 0

    def shifted(cur, prev8, start_row, mu):
        rows = cur.shape[0]
        prev_row = jnp.where(first, start_row, prev8[7:8, :])
        rolled = pltpu.roll(cur, 1, 0) if rows > 1 else cur
        ridx = lax.broadcasted_iota(jnp.int32, cur.shape, 0)
        prev = jnp.where(ridx == 0, prev_row, rolled)
        return cur + mu * (prev - cur)

    zm = shifted(z_ref[...], zp_ref[...], sh_ref[0], mu_ref[...])
    sm = shifted(zs_ref[...], zsp_ref[...], shs_ref[0], mus_ref[...])
    r, k, v = zm[:, :wb], zm[:, wb:2 * wb], zm[:, 2 * wb:]
    wl = sm[:, LANES:2 * LANES]
    al = sm[:, 2 * LANES:3 * LANES]
    gl = sm[:, 3 * LANES:]
    w_pre = w0_ref[...] + _dot(jnp.tanh(wl).astype(BF16), w2_ref[...])
    ld_ref[...] = -jnp.exp(-_softplus(-w_pre) - 0.5)
    a = jax.nn.sigmoid(a0_ref[...] + _dot(al.astype(BF16), a2_ref[...]))
    g_ref[...] = _dot(jax.nn.sigmoid(gl).astype(BF16), g2_ref[...])
    r_ref[...] = r
    v_ref[...] = v
    a_ref[...] = a
    kk_ref[...] = k * kks_ref[...]
    k_ref[...] = k * (1.0 + (a - 1.0) * ka_ref[...])


def _rwkv_prep(zrkv, zs, shift_rkv, shift_s, mu_rkv, mu_s, w0, w2, a0, a2, g2, kk_scale, k_a, batch):
    bt, w3 = zrkv.shape
    wb = w3 // 3
    t = bt // batch
    ns = zs.shape[1]
    lw, la = w2.shape[0], a2.shape[0]
    tm = _tile(t, (128, 64, 32, 16, 8))
    bps = t // tm
    pad_rows = lambda w: jnp.zeros((LANES, wb), BF16).at[:w.shape[0]].set(w.astype(BF16))
    row = lambda n: pl.BlockSpec((tm, n), lambda i: (i, 0))
    prev = lambda n: pl.BlockSpec((8, n), lambda i: (jnp.maximum(i * (tm // 8) - 1, 0), 0))
    seq = lambda n: pl.BlockSpec((1, 1, n), lambda i: (i // bps, 0, 0))
    vec = lambda n: pl.BlockSpec((1, n), lambda i: (0, 0))
    mat = lambda r, c: pl.BlockSpec((r, c), lambda i: (0, 0))
    outs = pl.pallas_call(
        functools.partial(_rwkv_prep_kernel, blocks_per_seq=bps, wb=wb, lw=lw, la=la),
        grid=(bt // tm,),
        in_specs=[row(w3), prev(w3), row(ns), prev(ns), seq(w3), seq(ns), vec(w3), vec(ns),
                  vec(wb), mat(LANES, wb), vec(wb), mat(LANES, wb), mat(g2.shape[0], wb), vec(wb), vec(wb)],
        out_specs=[row(wb)] * 7,
        out_shape=[jax.ShapeDtypeStruct((bt, wb), F32)] * 7,
        compiler_params=_params(1),
        name="rwkv_prep",
    )(zrkv, zrkv, zs, zs, shift_rkv.reshape(batch, 1, w3), shift_s.reshape(batch, 1, ns),
      mu_rkv.reshape(1, w3), mu_s.reshape(1, ns), w0.reshape(1, wb), pad_rows(w2), a0.reshape(1, wb),
      pad_rows(a2), g2.astype(BF16), kk_scale.reshape(1, wb), k_a.reshape(1, wb))
    return outs


def _rwkv_kernel(r_ref, ld_ref, k_ref, v_ref, kk_ref, a_ref, g_ref, rk_ref, gng_ref, gnb_ref, s0_ref,
                 y_ref, s1_ref, s_s, *, chunk, n_chunks, hs):
    tb = pl.program_id(2)

    @pl.when(tb == 0)
    def _():
        s_s[...] = s0_ref[0, 0]

    C = chunk
    R = 2 * C
    W = 2 * hs
    lane = lax.broadcasted_iota(jnp.int32, (R, W), 1)
    rowi = lax.broadcasted_iota(jnp.int32, (R, W), 0)
    headmask = (_div_pow2(rowi, C) == _div_pow2(lane, hs)).astype(F32)
    ri = lax.broadcasted_iota(jnp.int32, (R, R), 0)
    ci_ = lax.broadcasted_iota(jnp.int32, (R, R), 1)
    same = _div_pow2(ri, C) == _div_pow2(ci_, C)
    lower_incl = (same & (ci_ <= ri)).astype(F32)
    lower_strict = (same & (ci_ < ri)).astype(F32)
    eye_r = (ri == ci_).astype(F32)
    wi = lax.broadcasted_iota(jnp.int32, (W, W), 0)
    wj = lax.broadcasted_iota(jnp.int32, (W, W), 1)
    eye_w = (wi == wj).astype(F32)
    lane_row = lax.broadcasted_iota(jnp.int32, (1, W), 1)
    rk = rk_ref[...]
    gng = gng_ref[...]
    gnb = gnb_ref[...]
    chunks = range(n_chunks)
    rows = [slice(c * C, (c + 1) * C) for c in chunks]
    bf = lambda x: x.astype(BF16)

    def stack(ref):
        return [jnp.concatenate([ref[rows[c], :]] * 2, axis=0) * headmask for c in chunks]

    rs, lds, ks, vs, kks, as_ = (stack(ref) for ref in (r_ref, ld_ref, k_ref, v_ref, kk_ref, a_ref))
    lam = [jnp.dot(lower_incl, lds[c], precision=HIGHEST, preferred_element_type=F32) for c in chunks]
    e_pos = [jnp.exp(lam[c]) for c in chunks]
    e_neg = [jnp.exp(-lam[c]) for c in chunks]
    kkn = [kks[c] * lax.rsqrt(jnp.maximum(jnp.sum(kks[c] * kks[c], axis=-1, keepdims=True), 1e-24)) for c in chunks]
    kap = [bf(kkn[c] * jnp.exp(lam[c] - lds[c])) for c in chunks]
    bt = [bf(kkn[c] * as_[c] * e_neg[c]) for c in chunks]
    kt = [bf(ks[c] * e_neg[c]) for c in chunks]
    rho = [rs[c] * e_pos[c] for c in chunks]
    vb = [bf(vs[c]) for c in chunks]
    wide = R == LANES
    if wide:
        gram = [_dot_nt(jnp.concatenate([kap[c], bf(rho[c])], axis=0), jnp.concatenate([bt[c], kt[c]], axis=0))
                for c in chunks]
        a_kb = [gram[c][:R, :R] * lower_strict for c in chunks]
        a_kk = [bf(gram[c][:R, R:] * lower_strict) for c in chunks]
        a_rb = [bf(gram[c][R:, :R] * lower_incl) for c in chunks]
        a_rk = [bf(gram[c][R:, R:] * lower_incl) for c in chunks]
    else:
        a_kb = [_dot_nt(kap[c], bt[c]) * lower_strict for c in chunks]
        a_kk = [bf(_dot_nt(kap[c], kt[c]) * lower_strict) for c in chunks]
        a_rb = [bf(_dot_nt(bf(rho[c]), bt[c]) * lower_incl) for c in chunks]
        a_rk = [bf(_dot_nt(bf(rho[c]), kt[c]) * lower_incl) for c in chunks]
    pair = (_div_pow2(ri, 2) == _div_pow2(ci_, 2)).astype(F32)
    tinv = [eye_r - a_kb[c] * pair for c in chunks]
    bs = 4
    while bs <= C:
        lower_left = ((_div_pow2(ri, bs) == _div_pow2(ci_, bs)) & (_mod_pow2(ri, bs) >= bs // 2)
                      & (_mod_pow2(ci_, bs) < bs // 2)).astype(F32)
        t16 = [bf(tinv[c]) for c in chunks]
        left = [bf(_dot(t16[c], bf(a_kb[c] * lower_left))) for c in chunks]
        tinv = [tinv[c] - _dot(left[c], t16[c]) for c in chunks]
        bs *= 2
    t16 = [bf(tinv[c]) for c in chunks]
    av = [bf(_dot(a_kk[c], vb[c])) for c in chunks]
    if wide:
        ku = [_dot(t16[c], jnp.concatenate([kap[c], av[c]], axis=1)) for c in chunks]
        kap2 = [bf(ku[c][:, :W]) for c in chunks]
        u0 = [bf(ku[c][:, W:]) for c in chunks]
    else:
        kap2 = [bf(_dot(t16[c], kap[c])) for c in chunks]
        u0 = [bf(_dot(t16[c], av[c])) for c in chunks]
    gamma = [jnp.where(lane_row < hs, e_pos[c][C - 1:C, :], e_pos[c][R - 1:R, :]) for c in chunks]
    m_p = [(eye_w - _dot_tn(kap2[c], bt[c])) * gamma[c] for c in chunks]
    g_p = [_dot_tn(jnp.concatenate([vb[c], u0[c]], axis=0), jnp.concatenate([kt[c], -bt[c]], axis=0)) * gamma[c]
           for c in chunks]
    if wide:
        ru = [_dot(a_rb[c], jnp.concatenate([kap2[c], u0[c]], axis=1)) for c in chunks]
        rho2 = [rho[c] - ru[c][:, :W] for c in chunks]
        y0 = [_dot(a_rk[c], vb[c]) - ru[c][:, W:] for c in chunks]
    else:
        rho2 = [rho[c] - _dot(a_rb[c], kap2[c]) for c in chunks]
        y0 = [_dot(a_rk[c], vb[c]) - _dot(a_rb[c], u0[c]) for c in chunks]
    bonus_v = [jnp.sum(rs[c] * ks[c] * rk, axis=-1, keepdims=True) * vs[c] for c in chunks]

    s = s_s[...]
    for c in chunks:
        ys = lax.dot_general(rho2[c], s, (((1,), (1,)), ((), ())), precision=HIGHEST,
                             preferred_element_type=F32) + y0[c]
        s = jnp.dot(s, m_p[c], precision=HIGHEST, preferred_element_type=F32) + g_p[c]
        mean = jnp.sum(ys, axis=-1, keepdims=True) / hs
        d = (ys - mean) * headmask
        var = jnp.sum(d * d, axis=-1, keepdims=True) / hs
        yn = (d * lax.rsqrt(var + GN_EPS_B) * gng + gnb) * headmask + bonus_v[c]
        y_ref[rows[c], :] = ((yn[:C] + yn[C:]) * g_ref[rows[c], :]).astype(y_ref.dtype)
    s_s[...] = s

    @pl.when(tb == pl.num_programs(2) - 1)
    def _():
        s1_ref[0, 0] = s


def _rwkv(r, ld, k, v, kk, a, g, r_k, gn_g, gn_b, s0, batch, out_dtype):
    bt, wb = r.shape
    t = bt // batch
    nh, hs = s0.shape[1], s0.shape[2]
    assert 2 * hs == LANES and nh % 2 == 0
    npair = nh // 2
    chunk = math.gcd(t, CHUNK_B)
    tblk = _tile(t, (512, 256, 128, 64))
    ntb = t // tblk
    s0p = s0.reshape(batch, npair, 2, hs, hs)
    zero = jnp.zeros_like(s0p[:, :, 0])
    s0bd = jnp.concatenate([jnp.concatenate([s0p[:, :, 0], zero], axis=-1),
                            jnp.concatenate([zero, s0p[:, :, 1]], axis=-1)], axis=-2)
    blk = pl.BlockSpec((tblk, LANES), lambda b, p, i: (b * ntb + i, p))
    vec = pl.BlockSpec((1, LANES), lambda b, p, i: (0, p))
    st = pl.BlockSpec((1, 1, LANES, LANES), lambda b, p, i: (b, p, 0, 0))
    y, s1bd = pl.pallas_call(
        functools.partial(_rwkv_kernel, chunk=chunk, n_chunks=tblk // chunk, hs=hs),
        grid=(batch, npair, ntb),
        in_specs=[blk] * 7 + [vec] * 3 + [st],
        out_specs=[blk, st],
        out_shape=[jax.ShapeDtypeStruct((bt, wb), out_dtype),
                   jax.ShapeDtypeStruct((batch, npair, LANES, LANES), F32)],
        scratch_shapes=[pltpu.VMEM((LANES, LANES), F32)],
        compiler_params=_params(3),
        name="rwkv",
    )(r, ld, k, v, kk, a, g, r_k.reshape(1, wb), gn_g.reshape(1, wb), gn_b.reshape(1, wb), s0bd)
    s1 = jnp.stack([s1bd[:, :, :hs, :hs], s1bd[:, :, hs:, hs:]], axis=2).reshape(batch, nh, hs, hs)
    return y, s1


def _lf_cum_kernel(z_ref, b_ref, lf_ref, g_ref, carry_s):
    @pl.when(pl.program_id(1) == 0)
    def _():
        carry_s[...] = jnp.zeros_like(carry_s)

    lf = _log_sigmoid(z_ref[...] + b_ref[...])
    rows = lf.shape[0]
    ti = lax.broadcasted_iota(jnp.int32, (rows, rows), 0)
    si = lax.broadcasted_iota(jnp.int32, (rows, rows), 1)
    cum = jnp.dot((si <= ti).astype(F32), lf, precision=HIGHEST, preferred_element_type=F32) + carry_s[...]
    lf_ref[...] = lf
    g_ref[...] = cum
    carry_s[...] = cum[rows - 1:rows, :]


def _lf_cum(zf, b_f, batch):
    bt = zf.shape[0]
    t = bt // batch
    tb = _tile(t, (256, 128, 64, 32, 16, 8))
    ntb = t // tb
    bpad = jnp.zeros((1, LANES), F32).at[0, :b_f.shape[0]].set(b_f)
    blk = pl.BlockSpec((tb, LANES), lambda b, i: (b * ntb + i, 0))
    return pl.pallas_call(
        _lf_cum_kernel,
        grid=(batch, ntb),
        in_specs=[blk, pl.BlockSpec((1, LANES), lambda b, i: (0, 0))],
        out_specs=[blk, blk],
        out_shape=[jax.ShapeDtypeStruct((bt, LANES), F32)] * 2,
        scratch_shapes=[pltpu.VMEM((1, LANES), F32)],
        compiler_params=_params(2),
        name="lf_cum",
    )(zf, bpad)


def _fox_kernel(q_ref, k_ref, v_ref, gq_ref, gk_ref, o_ref, kb_s, vb_s, *, bq, nq, scale):
    qi = pl.program_id(2)

    @pl.when(qi == 0)
    def _():
        kb_s[...] = k_ref[...].astype(BF16)
        vb_s[...] = v_ref[...].astype(BF16)

    q = q_ref[...].astype(BF16)
    gq = gq_ref[0, 0]
    ti = lax.broadcasted_iota(jnp.int32, (bq, bq), 0)
    si = lax.broadcasted_iota(jnp.int32, (bq, bq), 1)

    for c in range(nq):
        @pl.when(qi == c)
        def _(c=c):
            past = c * bq
            sd = _dot_nt(q, kb_s[past:past + bq, :]) * scale + (gq - gk_ref[0, 0, :, past:past + bq])
            sd = jnp.where(si <= ti, sd, NEG_INF)
            m = jnp.max(sd, axis=-1, keepdims=True)
            if past:
                sp = _dot_nt(q, kb_s[0:past, :]) * scale + (gq - gk_ref[0, 0, :, 0:past])
                m = jnp.maximum(m, jnp.max(sp, axis=-1, keepdims=True))
            pd = jnp.exp(sd - m)
            l = jnp.sum(pd, axis=-1, keepdims=True)
            acc = _dot(pd.astype(BF16), vb_s[past:past + bq, :])
            if past:
                pp = jnp.exp(sp - m)
                l = l + jnp.sum(pp, axis=-1, keepdims=True)
                acc = acc + _dot(pp.astype(BF16), vb_s[0:past, :])
            o_ref[...] = (acc / l).astype(o_ref.dtype)


def _fox_prompt(q, k, v, g, batch, nh):
    bt, width = q.shape
    t = bt // batch
    dh = width // nh
    bq = _tile(t, (256, 128))
    nq = t // bq
    gq = jnp.transpose(g, (0, 2, 1)).reshape(batch, nh, t, 1)
    gk = jnp.transpose(g, (0, 2, 1)).reshape(batch, nh, 1, t)
    return pl.pallas_call(
        functools.partial(_fox_kernel, bq=bq, nq=nq, scale=dh ** -0.5),
        grid=(batch, nh, nq),
        in_specs=[pl.BlockSpec((bq, dh), lambda b, h, i: (b * nq + i, h)),
                  pl.BlockSpec((t, dh), lambda b, h, i: (b, h)),
                  pl.BlockSpec((t, dh), lambda b, h, i: (b, h)),
                  pl.BlockSpec((1, 1, bq, 1), lambda b, h, i: (b, h, i, 0)),
                  pl.BlockSpec((1, 1, 1, t), lambda b, h, i: (b, h, 0, 0))],
        out_specs=pl.BlockSpec((bq, dh), lambda b, h, i: (b * nq + i, h)),
        out_shape=jax.ShapeDtypeStruct((bt, width), BF16),
        scratch_shapes=[pltpu.VMEM((t, dh), BF16), pltpu.VMEM((t, dh), BF16)],
        compiler_params=_params(3),
        name="fox_prompt",
    )(q, k, v, gq, gk)


def _gpast_kernel(pt_ref, lf_ref, o_ref, carry_s):
    del pt_ref

    @pl.when(pl.program_id(1) == 0)
    def _():
        carry_s[...] = jnp.zeros_like(carry_s)

    lf = lf_ref[0, 0].astype(F32)
    rows = lf.shape[0]
    ti = lax.broadcasted_iota(jnp.int32, (rows, rows), 0)
    si = lax.broadcasted_iota(jnp.int32, (rows, rows), 1)
    later = jnp.dot((si > ti).astype(F32), lf, precision=HIGHEST, preferred_element_type=F32)
    o_ref[0, 0] = -(later + carry_s[...])
    carry_s[...] = carry_s[...] + jnp.sum(lf, axis=0, keepdims=True)


def _gpast(cache_logf, layer, page_table):
    batch, npg = page_table.shape
    _, _, page, nh = cache_logf.shape
    grid_spec = pltpu.PrefetchScalarGridSpec(
        num_scalar_prefetch=1,
        grid=(batch, npg),
        in_specs=[pl.BlockSpec((1, 1, page, nh), lambda b, p, pt: (layer, pt[b, npg - 1 - p], 0, 0))],
        out_specs=pl.BlockSpec((1, 1, page, nh), lambda b, p, pt: (b, npg - 1 - p, 0, 0)),
        scratch_shapes=[pltpu.VMEM((1, nh), F32)],
    )
    return pl.pallas_call(
        _gpast_kernel,
        grid_spec=grid_spec,
        out_shape=jax.ShapeDtypeStruct((batch, npg, page, nh), F32),
        compiler_params=_params(2),
        name="gpast",
    )(page_table, cache_logf)


HEAD_GROUP = 8


def _fox_dec_kernel(pt_ref, q_ref, kp_ref, vp_ref, gp_ref, kn_ref, vn_ref, gq_ref, gn_ref, o_ref,
                    base_s, m_s, l_s, acc_s, *, scale):
    del pt_ref
    p = pl.program_id(1)
    hg = HEAD_GROUP
    ng, nq, dh = q_ref.shape[1:]
    nk = kp_ref.shape[2] * hg

    @pl.when(p == 0)
    def _():
        qi = lax.broadcasted_iota(jnp.int32, (nq, nk), 0)
        ki = lax.broadcasted_iota(jnp.int32, (nq, nk), 1)
        same_head = _mod_pow2(qi, hg) == _mod_pow2(ki, hg)
        for g in range(ng):
            base_s[g] = jnp.where(same_head, gq_ref[0, g], NEG_INF)
        m_s[...] = jnp.full_like(m_s, NEG_INF)
        l_s[...] = jnp.zeros_like(l_s)
        acc_s[...] = jnp.zeros_like(acc_s)

    groups = range(ng)

    def update(s, v):
        m_old = [m_s[g] for g in groups]
        m_new = [jnp.maximum(m_old[g], jnp.max(s[g], axis=-1, keepdims=True)) for g in groups]
        pr = [jnp.exp(s[g] - m_new[g]) for g in groups]
        pv = [_dot(pr[g].astype(BF16), v[g]) for g in groups]
        for g in groups:
            alpha = jnp.exp(m_old[g] - m_new[g])
            l_s[g] = alpha * l_s[g] + jnp.sum(pr[g], axis=-1, keepdims=True)
            acc_s[g] = alpha * acc_s[g] + pv[g]
            m_s[g] = m_new[g]

    heads = [slice(g * hg, (g + 1) * hg) for g in groups]
    q = [q_ref[0, g].astype(BF16) for g in groups]
    k = [kp_ref[0, 0, :, heads[g], :].reshape(nk, dh).astype(BF16) for g in groups]
    v = [vp_ref[0, 0, :, heads[g], :].reshape(nk, dh).astype(BF16) for g in groups]
    update([_dot_nt(q[g], k[g]) * scale + (base_s[g] - gp_ref[0, 0, g]) for g in groups], v)

    @pl.when(p == pl.num_programs(1) - 1)
    def _():
        qi = lax.broadcasted_iota(jnp.int32, (nq, nq), 0)
        ki = lax.broadcasted_iota(jnp.int32, (nq, nq), 1)
        keep = (_mod_pow2(qi, hg) == _mod_pow2(ki, hg)) & (_div_pow2(ki, hg) <= _div_pow2(qi, hg))
        s = [_dot_nt(q[g], kn_ref[0, g].astype(BF16)) * scale + (gq_ref[0, g] - gn_ref[0, g]) for g in groups]
        update([jnp.where(keep, s[g], NEG_INF) for g in groups], [vn_ref[0, g].astype(BF16) for g in groups])
        for g in groups:
            o_ref[0, g] = (acc_s[g] / l_s[g]).astype(o_ref.dtype)


def _fox_decode(q, kn, vn, g_new, cache_k, cache_v, g_past, layer, page_table):
    batch, ts, nh, dh = q.shape
    npg = page_table.shape[1]
    page = cache_k.shape[2]
    hg = HEAD_GROUP
    assert nh % hg == 0
    ng = nh // hg
    rows = ts * hg

    def grouped(x):
        n = x.shape[-1]
        return jnp.transpose(x.reshape(batch, ts, ng, hg, n), (0, 2, 1, 3, 4)).reshape(batch, ng, rows, n)

    gq = grouped(g_new[..., None])
    gn = gq.reshape(batch, ng, 1, rows)
    gp = jnp.transpose(g_past.reshape(batch, npg, page, ng, hg), (0, 1, 3, 2, 4)).reshape(batch, npg, ng, 1, page * hg)
    new = lambda r, n: pl.BlockSpec((1, ng, r, n), lambda b, p, pt: (b, 0, 0, 0))
    pg = pl.BlockSpec((1, 1, page, nh, dh), lambda b, p, pt: (layer, pt[b, p], 0, 0, 0))
    grid_spec = pltpu.PrefetchScalarGridSpec(
        num_scalar_prefetch=1,
        grid=(batch, npg),
        in_specs=[new(rows, dh), pg, pg,
                  pl.BlockSpec((1, 1, ng, 1, page * hg), lambda b, p, pt: (b, p, 0, 0, 0)),
                  new(rows, dh), new(rows, dh), new(rows, 1), new(1, rows)],
        out_specs=new(rows, dh),
        scratch_shapes=[pltpu.VMEM((ng, rows, page * hg), F32), pltpu.VMEM((ng, rows, 1), F32),
                        pltpu.VMEM((ng, rows, 1), F32), pltpu.VMEM((ng, rows, dh), F32)],
    )
    out = pl.pallas_call(
        functools.partial(_fox_dec_kernel, scale=dh ** -0.5),
        grid_spec=grid_spec,
        out_shape=jax.ShapeDtypeStruct((batch, ng, rows, dh), F32),
        compiler_params=_params(2),
        name="fox_decode",
    )(page_table, grouped(q), cache_k, cache_v, gp, grouped(kn), grouped(vn), gq, gn)
    return jnp.transpose(out.reshape(batch, ng, ts, hg, dh), (0, 2, 1, 3, 4)).reshape(batch, ts, nh, dh)


def _even_layer(xb, batch, st, w, big):
    c0, n0, m0, s0, shift0 = st
    nh_a, dh_a = c0.shape[1], c0.shape[2]
    wa = nh_a * dh_a
    bt = xb.shape[0]
    t = bt // batch
    act = BF16 if big else F32
    z_a = _mm(xb, w["w_qkvo"], act)
    zs = _mm(xb, w["w_small"], F32)
    zrkv = _mm(xb, w["w_rkv"], F32)
    gates = _mlstm_gates(zs, w["b_gate"], nh_a)
    h, c1, n1, m1 = _mlstm(z_a, gates, w["g_head"], c0, n0, m0, batch, act)
    wb3 = zrkv.shape[1]
    lw, la, lg = w["lw"], w["la"], w["lg"]
    sh_rkv = shift0[:, :wb3]
    sh_s = jnp.zeros((batch, 3 * LANES + lg), F32)
    sh_s = sh_s.at[:, LANES:LANES + lw].set(shift0[:, wb3:wb3 + lw])
    sh_s = sh_s.at[:, 2 * LANES:2 * LANES + la].set(shift0[:, wb3 + lw:wb3 + lw + la])
    sh_s = sh_s.at[:, 3 * LANES:].set(shift0[:, wb3 + lw + la:])
    r, ld, k, v, kk, a, g = _rwkv_prep(zrkv, zs, sh_rkv, sh_s, w["mu_rkv"], w["mu_s"], w["w0"], w["w2"],
                                       w["a0"], w["a2"], w["g2"], w["kk_scale"], w["k_a"], batch)
    y, s1 = _rwkv(r, ld, k, v, kk, a, g, w["r_k"], w["gn_g"], w["gn_b"], s0, batch, act)
    mix = _mm2(h.astype(BF16), y.astype(BF16), w["w_out"])
    last_rkv = zrkv.reshape(batch, t, wb3)[:, -1]
    last_s = zs.reshape(batch, t, -1)[:, -1]
    shift1 = jnp.concatenate([last_rkv, last_s[:, LANES:LANES + lw], last_s[:, 2 * LANES:2 * LANES + la],
                              last_s[:, 3 * LANES:]], axis=-1)
    return mix, (c1, n1, m1, s1, shift1)


def _fox_project(xb, w, big):
    q = _mm(xb, w["w_q"], BF16 if big else F32)
    k = _mm(xb, w["w_k"], F32)
    v = _mm(xb, w["w_v"], F32)
    zf = _mm(xb, w["w_f"], F32)
    return q, k, v, zf


def _post_block(x, mix, pb, w, alpha):
    x1, x1b = _add_ln(x, mix, w["ln_mix_g"], w["ln_mix_b"], alpha)
    f = _ffn(x1b, w["w_ffn_in"], w["w_ffn_out"])
    x2, x2b = _add_ln(x1, f, w["ln_ffn_g"], w["ln_ffn_b"], alpha)
    return _ple(x2, x2b, pb, w["w_ple_gate"], w["w_ple"])


def kernel(x_prompt, x_sample, cache_k, cache_v, cache_logf, state_mlstm_c, state_mlstm_n, state_mlstm_m, state_rwkv_wkv, state_rwkv_shift, page_table, p_prompt, p_sample, w_in_ab, b_gate_a, g_head_a, mu_b, w0_b, w2_b, a0_b, a2_b, g2_b, kk_scale_b, ka_b, rk_b, gn_g_b, gn_b_b, w_out_ab, w_in_c, b_f_c, w_out_c, ln_mix_g, ln_mix_b, ln_ffn_g, ln_ffn_b, w_ffn_in, w_ffn_out, w_ple, w_ple_gate):
    bp, tp, d = x_prompt.shape
    bs, ts, _ = x_sample.shape
    depth = ln_mix_g.shape[0]
    alpha = (2 * depth) ** 0.25
    nh_a, dh_a = state_mlstm_c.shape[2], state_mlstm_c.shape[3]
    wa = nh_a * dh_a
    nh_b, hs_b = state_rwkv_wkv.shape[2], state_rwkv_wkv.shape[3]
    wb = nh_b * hs_b
    lw, la, lg = w2_b.shape[1], a2_b.shape[1], g2_b.shape[1]
    nh_c, dh_c = cache_k.shape[3], cache_k.shape[4]
    n_in_a = 4 * wa + 2 * nh_a
    assert lw <= LANES and la <= LANES and 2 * nh_a <= LANES and nh_c <= LANES

    xp, xs = x_prompt.reshape(bp * tp, d), x_sample.reshape(bs * ts, d)
    xpb, xsb = xp.astype(BF16), xs.astype(BF16)
    pp = p_prompt.reshape(depth, bp * tp, -1).astype(BF16)
    ps = p_sample.reshape(depth, bs * ts, -1).astype(BF16)

    outs = {name: [] for name in ("kp", "vp", "lfp", "ks", "vs", "lfs", "cp", "np", "mp", "cs", "ns", "ms",
                                  "sp", "shp", "ss", "shs")}
    for i in range(depth):
        j = i // 2
        w = {"ln_mix_g": ln_mix_g[i], "ln_mix_b": ln_mix_b[i], "ln_ffn_g": ln_ffn_g[i], "ln_ffn_b": ln_ffn_b[i],
             "w_ffn_in": w_ffn_in[i].astype(BF16), "w_ffn_out": w_ffn_out[i].astype(BF16),
             "w_ple": w_ple[i].astype(BF16), "w_ple_gate": w_ple_gate[i].astype(BF16)}
        if i % 2 == 0:
            w_in = w_in_ab[j]
            rk0 = n_in_a
            lo = rk0 + 3 * wb
            zc = lambda n: jnp.zeros((d, n), F32)
            w_small = jnp.concatenate([w_in[:, 4 * wa:n_in_a], zc(LANES - 2 * nh_a),
                                       w_in[:, lo:lo + lw], zc(LANES - lw),
                                       w_in[:, lo + lw:lo + lw + la], zc(LANES - la),
                                       w_in[:, lo + lw + la:]], axis=1)
            mu = mu_b[j]
            zv = lambda n: jnp.zeros((n,), F32)
            mu_s = jnp.concatenate([zv(LANES), mu[3 * wb:3 * wb + lw], zv(LANES - lw),
                                    mu[3 * wb + lw:3 * wb + lw + la], zv(LANES - la), mu[3 * wb + lw + la:]])
            w.update({"w_qkvo": w_in[:, :4 * wa].astype(BF16), "w_small": w_small.astype(BF16),
                      "w_rkv": w_in[:, rk0:lo].astype(BF16), "b_gate": b_gate_a[j], "g_head": g_head_a[j],
                      "mu_rkv": mu[:3 * wb], "mu_s": mu_s, "w0": w0_b[j], "w2": w2_b[j], "a0": a0_b[j],
                      "a2": a2_b[j], "g2": g2_b[j], "kk_scale": kk_scale_b[j], "k_a": ka_b[j],
                      "r_k": rk_b[j].reshape(-1), "gn_g": gn_g_b[j], "gn_b": gn_b_b[j],
                      "w_out": w_out_ab[j].astype(BF16), "lw": lw, "la": la, "lg": lg})
            zero_state = (jnp.zeros((bp, nh_a, dh_a, dh_a), F32), jnp.zeros((bp, nh_a, dh_a), F32),
                          jnp.zeros((bp, nh_a), F32), jnp.zeros((bp, nh_b, hs_b, hs_b), F32),
                          jnp.zeros((bp, mu.shape[0]), F32))
            mix_p, (cp, np_, mp, sp, shp) = _even_layer(xpb, bp, zero_state, w, True)
            mix_s, (cs, ns, ms, ss, shs) = _even_layer(
                xsb, bs, (state_mlstm_c[j], state_mlstm_n[j], state_mlstm_m[j], state_rwkv_wkv[j],
                          state_rwkv_shift[j]), w, False)
            for name, val in (("cp", cp), ("np", np_), ("mp", mp), ("sp", sp), ("shp", shp),
                              ("cs", cs), ("ns", ns), ("ms", ms), ("ss", ss), ("shs", shs)):
                outs[name].append(val)
        else:
            w_in = w_in_c[j]
            w_f = jnp.concatenate([w_in[:, 3 * d:], jnp.zeros((d, LANES - nh_c), F32)], axis=1)
            w.update({"w_q": w_in[:, :d].astype(BF16), "w_k": w_in[:, d:2 * d].astype(BF16),
                      "w_v": w_in[:, 2 * d:3 * d].astype(BF16), "w_f": w_f.astype(BF16)})
            w_out = w_out_c[j].astype(BF16)
            qp, kp, vp, zfp = _fox_project(xpb, w, True)
            lfp, gp = _lf_cum(zfp, b_f_c[j], bp)
            lfp = lfp[:, :nh_c].reshape(bp, tp, nh_c)
            att_p = _fox_prompt(qp, kp, vp, gp[:, :nh_c].reshape(bp, tp, nh_c), bp, nh_c)
            mix_p = _mm(att_p, w_out, F32)
            qs, kn, vn, zfs = _fox_project(xsb, w, False)
            lfs, g_new = _lf_cum(zfs, b_f_c[j], bs)
            lfs = lfs[:, :nh_c].reshape(bs, ts, nh_c)
            g_past = _gpast(cache_logf, j, page_table)
            heads = lambda x: x.reshape(bs, ts, nh_c, dh_c)
            att_s = _fox_decode(heads(qs), heads(kn), heads(vn), g_new[:, :nh_c].reshape(bs, ts, nh_c),
                                cache_k, cache_v, g_past, j, page_table)
            mix_s = _mm(att_s.reshape(bs * ts, d).astype(BF16), w_out, F32)
            for name, val in (("kp", kp.reshape(bp, tp, nh_c, dh_c)), ("vp", vp.reshape(bp, tp, nh_c, dh_c)),
                              ("lfp", lfp), ("ks", kn.reshape(bs, ts, nh_c, dh_c)),
                              ("vs", vn.reshape(bs, ts, nh_c, dh_c)), ("lfs", lfs)):
                outs[name].append(val)
        xp, xpb = _post_block(xp, mix_p, pp[i], w, alpha)
        xs, xsb = _post_block(xs, mix_s, ps[i], w, alpha)
    stk = lambda name: jnp.stack(outs[name])
    return (xp.reshape(bp, tp, d), xs.reshape(bs, ts, d),
            stk("kp"), stk("vp"), stk("lfp"), stk("ks"), stk("vs"), stk("lfs"),
            stk("cp"), stk("np"), stk("mp"), stk("cs"), stk("ns"), stk("ms"),
            stk("sp"), stk("shp"), stk("ss"), stk("shs"))
```

```python
import functools
import math

import jax
import jax.numpy as jnp
from jax import lax
from jax.experimental import pallas as pl
from jax.experimental.pallas import tpu as pltpu

F32 = jnp.float32
BF16 = jnp.bfloat16
HIGHEST = lax.Precision.HIGHEST

LANES = 128
VMEM_LIMIT_BYTES = 56 * 1024 * 1024
PANEL_DOUBLE_BUFFER_BYTES = VMEM_LIMIT_BYTES // 4

CHUNK_A = 64
GATE_CAP = 15.0
HEAD_EPS_A = 1e-6
CHUNK_B = 64
GN_EPS_B = 64e-5
LN_EPS = 1e-5
NEG_INF = float("-inf")


def _tile(dim, prefs):
    for t in prefs:
        if t <= dim and dim % t == 0:
            return t
    return dim


def _params(n_axes):
    return pltpu.CompilerParams(dimension_semantics=("arbitrary",) * n_axes,
                                vmem_limit_bytes=VMEM_LIMIT_BYTES)


def _dot(a, b):
    return jnp.dot(a, b, preferred_element_type=F32)


def _dot_nt(a, b):
    return lax.dot_general(a, b, (((1,), (1,)), ((), ())), preferred_element_type=F32)


def _dot_tn(a, b):
    return lax.dot_general(a, b, (((0,), (0,)), ((), ())), preferred_element_type=F32)


def _div_pow2(x, n):
    assert n & (n - 1) == 0
    return lax.shift_right_logical(x, n.bit_length() - 1)


def _mod_pow2(x, n):
    assert n & (n - 1) == 0
    return jnp.bitwise_and(x, n - 1)


def _log_sigmoid(x):
    return jnp.minimum(x, 0.0) - jnp.log1p(jnp.exp(-jnp.abs(x)))


def _softplus(x):
    return jnp.maximum(x, 0.0) + jnp.log1p(jnp.exp(-jnp.abs(x)))


def _mm_kernel(x_ref, w_ref, o_ref):
    o_ref[...] = _dot(x_ref[...], w_ref[...]).astype(o_ref.dtype)


def _mm(x, w, out_dtype):
    m, k = x.shape
    n = w.shape[1]
    tm = _tile(m, (1024, 512, 256, 128, 64, 32, 16, 8))
    panel_bytes = tm * k * x.dtype.itemsize
    big_panel = panel_bytes > PANEL_DOUBLE_BUFFER_BYTES
    tn = _tile(n, (256, 128) if big_panel else (512, 384, 256, 128))
    x_spec = pl.BlockSpec((tm, k), lambda i, j: (i, 0), **({"pipeline_mode": pl.Buffered(1)} if big_panel else {}))
    return pl.pallas_call(
        _mm_kernel,
        grid=(m // tm, n // tn),
        in_specs=[x_spec,
                  pl.BlockSpec((k, tn), lambda i, j: (0, j))],
        out_specs=pl.BlockSpec((tm, tn), lambda i, j: (i, j)),
        out_shape=jax.ShapeDtypeStruct((m, n), out_dtype),
        compiler_params=_params(2),
        name="mm",
    )(x, w)


ROW_TILES = (1024, 512, 256, 128, 64, 32, 16, 8)
COL_TILES = (640, 512, 384, 256, 128)


def _w_spec(k, tn, layer, row_blk, col_blk0):
    return pl.BlockSpec((None, k, tn), lambda j, i: (layer, row_blk, col_blk0 + j))


def _first_row_block():
    return pl.program_id(1) == 0


def _proj_kernel(x_ref, xs_ref, w_ref, o_ref, os_ref, wb_s):
    @pl.when(_first_row_block())
    def _():
        wb_s[...] = w_ref[...].astype(BF16)
        os_ref[...] = _dot(xs_ref[...], wb_s[...]).astype(os_ref.dtype)

    o_ref[...] = _dot(x_ref[...], wb_s[...]).astype(o_ref.dtype)


def _proj(xp, xs, w, layer, col0, n, out_p, out_s=F32):
    m, k = xp.shape
    ms = xs.shape[0]
    tm = _tile(m, ROW_TILES)
    tn = _tile(n, COL_TILES)
    assert w.shape[1] == k and col0 % tn == 0
    return pl.pallas_call(
        _proj_kernel,
        grid=(n // tn, m // tm),
        in_specs=[pl.BlockSpec((tm, k), lambda j, i: (i, 0)),
                  pl.BlockSpec((ms, k), lambda j, i: (0, 0)),
                  _w_spec(k, tn, layer, 0, col0 // tn)],
        out_specs=[pl.BlockSpec((tm, tn), lambda j, i: (i, j)),
                   pl.BlockSpec((ms, tn), lambda j, i: (0, j))],
        out_shape=[jax.ShapeDtypeStruct((m, n), out_p), jax.ShapeDtypeStruct((ms, n), out_s)],
        scratch_shapes=[pltpu.VMEM((k, tn), BF16)],
        compiler_params=_params(2),
        name="proj",
    )(xp, xs, w)


def _proj2_kernel(x1_ref, x2_ref, x1s_ref, x2s_ref, w1_ref, w2_ref, o_ref, os_ref, w1b_s, w2b_s):
    @pl.when(_first_row_block())
    def _():
        w1b_s[...] = w1_ref[...].astype(BF16)
        w2b_s[...] = w2_ref[...].astype(BF16)
        os_ref[...] = _dot(x1s_ref[...], w1b_s[...]) + _dot(x2s_ref[...], w2b_s[...])

    o_ref[...] = _dot(x1_ref[...], w1b_s[...]) + _dot(x2_ref[...], w2b_s[...])


def _proj2(x1, x2, x1s, x2s, w, layer):
    m, k = x1.shape
    ms = x1s.shape[0]
    n = w.shape[2]
    assert x2.shape == (m, k) and w.shape[1] == 2 * k
    tm = _tile(m, ROW_TILES)
    tn = _tile(n, COL_TILES)
    rows = pl.BlockSpec((tm, k), lambda j, i: (i, 0))
    rows_s = pl.BlockSpec((ms, k), lambda j, i: (0, 0))
    return pl.pallas_call(
        _proj2_kernel,
        grid=(n // tn, m // tm),
        in_specs=[rows, rows, rows_s, rows_s, _w_spec(k, tn, layer, 0, 0), _w_spec(k, tn, layer, 1, 0)],
        out_specs=[pl.BlockSpec((tm, tn), lambda j, i: (i, j)),
                   pl.BlockSpec((ms, tn), lambda j, i: (0, j))],
        out_shape=[jax.ShapeDtypeStruct((m, n), F32), jax.ShapeDtypeStruct((ms, n), F32)],
        scratch_shapes=[pltpu.VMEM((k, tn), BF16), pltpu.VMEM((k, tn), BF16)],
        compiler_params=_params(2),
        name="proj2",
    )(x1, x2, x1s, x2s, w, w)


def _add_ln_kernel(x_ref, y_ref, g_ref, b_ref, o_ref, ob_ref, *, alpha):
    v = alpha * x_ref[...] + y_ref[...]
    mu = jnp.mean(v, axis=-1, keepdims=True)
    d = v - mu
    var = jnp.mean(d * d, axis=-1, keepdims=True)
    out = d * lax.rsqrt(var + LN_EPS) * g_ref[...] + b_ref[...]
    o_ref[...] = out
    ob_ref[...] = out.astype(BF16)


def _add_ln(x, y, g, b, alpha):
    m, d = x.shape
    tm = _tile(m, (256, 128, 64, 32, 16))
    row = pl.BlockSpec((tm, d), lambda i: (i, 0))
    vec = pl.BlockSpec((1, d), lambda i: (0, 0))
    return pl.pallas_call(
        functools.partial(_add_ln_kernel, alpha=alpha),
        grid=(m // tm,),
        in_specs=[row, row, vec, vec],
        out_specs=[row, row],
        out_shape=[jax.ShapeDtypeStruct((m, d), F32), jax.ShapeDtypeStruct((m, d), BF16)],
        compiler_params=_params(1),
        name="add_ln",
    )(x, y, g.reshape(1, d), b.reshape(1, d))


def _swiglu_kernel(x_ref, xs_ref, wg_ref, wu_ref, o_ref, os_ref, wgb_s, wub_s):
    def act(x):
        gate = _dot(x, wgb_s[...])
        return (gate * jax.nn.sigmoid(gate) * _dot(x, wub_s[...])).astype(BF16)

    @pl.when(_first_row_block())
    def _():
        wgb_s[...] = wg_ref[...].astype(BF16)
        wub_s[...] = wu_ref[...].astype(BF16)
        os_ref[...] = act(xs_ref[...])

    o_ref[...] = act(x_ref[...])


def _swiglu(xp, xs, w_in, layer):
    m, d = xp.shape
    ms = xs.shape[0]
    f = w_in.shape[2] // 2
    tm = _tile(m, ROW_TILES)
    tf = _tile(f, (256, 128))
    nf = f // tf
    return pl.pallas_call(
        _swiglu_kernel,
        grid=(nf, m // tm),
        in_specs=[pl.BlockSpec((tm, d), lambda j, i: (i, 0)),
                  pl.BlockSpec((ms, d), lambda j, i: (0, 0)),
                  _w_spec(d, tf, layer, 0, 0), _w_spec(d, tf, layer, 0, nf)],
        out_specs=[pl.BlockSpec((tm, tf), lambda j, i: (i, j)),
                   pl.BlockSpec((ms, tf), lambda j, i: (0, j))],
        out_shape=[jax.ShapeDtypeStruct((m, f), BF16), jax.ShapeDtypeStruct((ms, f), BF16)],
        scratch_shapes=[pltpu.VMEM((d, tf), BF16), pltpu.VMEM((d, tf), BF16)],
        compiler_params=_params(2),
        name="swiglu",
    )(xp, xs, w_in, w_in)


def _ple_kernel(xb_ref, xsb_ref, pb_ref, psb_ref, x_ref, xs_ref, wg_ref, wp_ref,
                o_ref, ob_ref, os_ref, osb_ref, wgb_s, wpb_s):
    def gated(x, xb, pb):
        return x + jax.nn.sigmoid(_dot(xb, wgb_s[...])) * _dot(pb, wpb_s[...])

    @pl.when(_first_row_block())
    def _():
        wgb_s[...] = wg_ref[...].astype(BF16)
        wpb_s[...] = wp_ref[...].astype(BF16)
        out_s = gated(xs_ref[...], xsb_ref[...], psb_ref[...])
        os_ref[...] = out_s
        osb_ref[...] = out_s.astype(BF16)

    out = gated(x_ref[...], xb_ref[...], pb_ref[...])
    o_ref[...] = out
    ob_ref[...] = out.astype(BF16)


def _ple(xp, xpb, pp, xs, xsb, ps, w_gate, w_ple, layer):
    m, d = xp.shape
    ms = xs.shape[0]
    dp = pp.shape[1]
    tm = _tile(m, ROW_TILES)
    tn = _tile(d, (512, 256, 128))
    tile = pl.BlockSpec((tm, tn), lambda j, i: (i, j))
    tile_s = pl.BlockSpec((ms, tn), lambda j, i: (0, j))
    return pl.pallas_call(
        _ple_kernel,
        grid=(d // tn, m // tm),
        in_specs=[pl.BlockSpec((tm, d), lambda j, i: (i, 0)),
                  pl.BlockSpec((ms, d), lambda j, i: (0, 0)),
                  pl.BlockSpec((tm, dp), lambda j, i: (i, 0)),
                  pl.BlockSpec((ms, dp), lambda j, i: (0, 0)),
                  tile, tile_s,
                  _w_spec(d, tn, layer, 0, 0), _w_spec(dp, tn, layer, 0, 0)],
        out_specs=[tile, tile, tile_s, tile_s],
        out_shape=[jax.ShapeDtypeStruct((m, d), F32), jax.ShapeDtypeStruct((m, d), BF16),
                   jax.ShapeDtypeStruct((ms, d), F32), jax.ShapeDtypeStruct((ms, d), BF16)],
        scratch_shapes=[pltpu.VMEM((d, tn), BF16), pltpu.VMEM((dp, tn), BF16)],
        compiler_params=_params(2),
        name="ple",
    )(xpb, xsb, pp, ps, xp, xs, w_gate, w_ple)


def _mlstm_gate_kernel(z_ref, b_ref, o_ref, *, nh):
    g = GATE_CAP * jnp.tanh((z_ref[...] + b_ref[...]) / GATE_CAP)
    lane = lax.broadcasted_iota(jnp.int32, g.shape, 1)
    o_ref[...] = jnp.where(lane < nh, g, _log_sigmoid(g))


def _mlstm_gates(zs, b_gate, nh):
    m = zs.shape[0]
    tm = _tile(m, (1024, 512, 256, 128, 64, 32, 16, 8))
    bpad = jnp.zeros((1, LANES), F32).at[0, :2 * nh].set(b_gate)
    return pl.pallas_call(
        functools.partial(_mlstm_gate_kernel, nh=nh),
        grid=(m // tm,),
        in_specs=[pl.BlockSpec((tm, LANES), lambda i: (i, 0)),
                  pl.BlockSpec((1, LANES), lambda i: (0, 0))],
        out_specs=pl.BlockSpec((tm, LANES), lambda i: (i, 0)),
        out_shape=jax.ShapeDtypeStruct((m, LANES), F32),
        compiler_params=_params(1),
        name="mlstm_gates",
    )(zs, bpad)


def _mlstm_kernel(q_ref, k_ref, v_ref, o_ref, gc_ref, gr_ref, gh_ref, c0_ref, n0_ref, m0_ref,
                  h_ref, c1_ref, n1_ref, m1_ref, c_s, n_s, m_s, *, chunk, n_chunks, scale):
    tb = pl.program_id(2)

    @pl.when(tb == 0)
    def _():
        c_s[...] = c0_ref[0, 0]
        n_s[...] = n0_ref[0, 0]
        m_s[...] = m0_ref[0, 0]

    L = chunk
    ti = lax.broadcasted_iota(jnp.int32, (L, L), 0)
    si = lax.broadcasted_iota(jnp.int32, (L, L), 1)
    causal = si <= ti
    gc = gc_ref[0, 0]
    gr = gr_ref[0, 0]
    c = c_s[...]
    n = n_s[...]
    m = m_s[...]
    for ci in range(n_chunks):
        sl = slice(ci * L, (ci + 1) * L)
        q = q_ref[sl, :].astype(BF16)
        ks = k_ref[sl, :].astype(F32) * scale
        v = v_ref[sl, :].astype(BF16)
        ig_col, lf_col = gc[sl, 0:1], gc[sl, 1:2]
        ig_row, lf_row = gr[0:1, sl], gr[1:2, sl]
        bcum_col = jnp.sum(jnp.where(causal, lf_row, 0.0), axis=1, keepdims=True)
        bcum_row = jnp.sum(jnp.where(ti <= si, lf_col, 0.0), axis=0, keepdims=True)
        inter = bcum_col + m
        logd = jnp.where(causal, bcum_col - bcum_row + ig_row, NEG_INF)
        m_t = jnp.maximum(inter, jnp.max(logd, axis=1, keepdims=True))
        w_carry = jnp.exp(inter - m_t)
        s = _dot_nt(q, ks.astype(BF16)) * jnp.exp(logd - m_t)
        num = _dot(s.astype(BF16), v) + w_carry * _dot(q, c.astype(BF16))
        qn = jnp.sum(q.astype(F32) * n, axis=1, keepdims=True)
        den = jnp.sum(s, axis=1, keepdims=True) + w_carry * qn
        h = num / jnp.maximum(jnp.abs(den), jnp.exp(-m_t))
        m_new = m_t[L - 1:L, :]
        w_c = jnp.exp(inter[L - 1:L, :] - m_new)
        w_rows = jnp.exp(bcum_col[L - 1:L, :] - bcum_col + ig_col - m_new)
        kw = w_rows * ks
        c = w_c * c + _dot_tn(kw.astype(BF16), v)
        n = w_c * n + jnp.sum(kw, axis=0, keepdims=True)
        m = m_new
        mu = jnp.mean(h, axis=-1, keepdims=True)
        d = h - mu
        var = jnp.mean(d * d, axis=-1, keepdims=True)
        hn = d * lax.rsqrt(var + HEAD_EPS_A) * gh_ref[0] * jax.nn.sigmoid(o_ref[sl, :].astype(F32))
        h_ref[sl, :] = hn.astype(h_ref.dtype)
    c_s[...] = c
    n_s[...] = n
    m_s[...] = m

    @pl.when(tb == pl.num_programs(2) - 1)
    def _():
        c1_ref[0, 0] = c
        n1_ref[0, 0] = n
        m1_ref[0, 0] = m


def _mlstm(z, gates, g_head, c0, n0, m0, batch, out_dtype):
    bt = z.shape[0]
    t = bt // batch
    nh, dh = c0.shape[1], c0.shape[2]
    chunk = math.gcd(t, CHUNK_A)
    tblk = _tile(t, (256, 128, 64))
    ntb = t // tblk
    g = gates[:, :2 * nh].reshape(batch, t, 2, nh)
    g_col = jnp.transpose(g, (0, 3, 1, 2))
    g_row = jnp.transpose(g, (0, 3, 2, 1))

    def col(off):
        return pl.BlockSpec((tblk, dh), lambda b, h, i: (b * ntb + i, off + h))

    state = lambda r, c: pl.BlockSpec((1, 1, r, c), lambda b, h, i: (b, h, 0, 0))
    h, c1, n1, m1 = pl.pallas_call(
        functools.partial(_mlstm_kernel, chunk=chunk, n_chunks=tblk // chunk, scale=dh ** -0.5),
        grid=(batch, nh, ntb),
        in_specs=[col(0), col(nh), col(2 * nh), col(3 * nh),
                  pl.BlockSpec((1, 1, tblk, 2), lambda b, h, i: (b, h, i, 0)),
                  pl.BlockSpec((1, 1, 2, tblk), lambda b, h, i: (b, h, 0, i)),
                  pl.BlockSpec((1, 1, dh), lambda b, h, i: (h, 0, 0)),
                  state(dh, dh), state(1, dh), state(1, 1)],
        out_specs=[col(0), state(dh, dh), state(1, dh), state(1, 1)],
        out_shape=[jax.ShapeDtypeStruct((bt, nh * dh), out_dtype),
                   jax.ShapeDtypeStruct((batch, nh, dh, dh), F32),
                   jax.ShapeDtypeStruct((batch, nh, 1, dh), F32),
                   jax.ShapeDtypeStruct((batch, nh, 1, 1), F32)],
        scratch_shapes=[pltpu.VMEM((dh, dh), F32), pltpu.VMEM((1, dh), F32), pltpu.VMEM((1, 1), F32)],
        compiler_params=_params(3),
        name="mlstm",
    )(z, z, z, z, g_col, g_row, g_head.reshape(nh, 1, dh), c0,
      n0.reshape(batch, nh, 1, dh), m0.reshape(batch, nh, 1, 1))
    return h, c1, n1.reshape(batch, nh, dh), m1.reshape(batch, nh)


def _rwkv_prep_kernel(z_ref, zp_ref, zs_ref, zsp_ref, sh_ref, shs_ref, mu_ref, mus_ref,
                      w0_ref, w2_ref, a0_ref, a2_ref, g2_ref, kks_ref, ka_ref,
                      r_ref, ld_ref, k_ref, v_ref, kk_ref, a_ref, g_ref, *, blocks_per_seq, wb, lw, la):
    i = pl.program_id(0)
    first = (i % blocks_per_seq) == 0

    def shifted(cur, prev8, start_row, mu):
        rows = cur.shape[0]
        prev_row = jnp.where(first, start_row, prev8[7:8, :])
        rolled = pltpu.roll(cur, 1, 0) if rows > 1 else cur
        ridx = lax.broadcasted_iota(jnp.int32, cur.shape, 0)
        prev = jnp.where(ridx == 0, prev_row, rolled)
        return cur + mu * (prev - cur)

    zm = shifted(z_ref[...], zp_ref[...], sh_ref[0], mu_ref[...])
    sm = shifted(zs_ref[...], zsp_ref[...], shs_ref[0], mus_ref[...])
    r, k, v = zm[:, :wb], zm[:, wb:2 * wb], zm[:, 2 * wb:]
    wl = sm[:, LANES:2 * LANES]
    al = sm[:, 2 * LANES:3 * LANES]
    gl = sm[:, 3 * LANES:]
    w_pre = w0_ref[...] + _dot(jnp.tanh(wl).astype(BF16), w2_ref[...])
    ld_ref[...] = -jnp.exp(-_softplus(-w_pre) - 0.5)
    a = jax.nn.sigmoid(a0_ref[...] + _dot(al.astype(BF16), a2_ref[...]))
    g_ref[...] = _dot(jax.nn.sigmoid(gl).astype(BF16), g2_ref[...])
    r_ref[...] = r
    v_ref[...] = v
    a_ref[...] = a
    kk_ref[...] = k * kks_ref[...]
    k_ref[...] = k * (1.0 + (a - 1.0) * ka_ref[...])


def _rwkv_prep(zrkv, zs, shift_rkv, shift_s, mu_rkv, mu_s, w0, w2, a0, a2, g2, kk_scale, k_a, batch):
    bt, w3 = zrkv.shape
    wb = w3 // 3
    t = bt // batch
    ns = zs.shape[1]
    lw, la = w2.shape[0], a2.shape[0]
    tm = _tile(t, (128, 64, 32, 16, 8))
    bps = t // tm
    pad_rows = lambda w: jnp.zeros((LANES, wb), BF16).at[:w.shape[0]].set(w.astype(BF16))
    row = lambda n: pl.BlockSpec((tm, n), lambda i: (i, 0))
    prev = lambda n: pl.BlockSpec((8, n), lambda i: (jnp.maximum(i * (tm // 8) - 1, 0), 0))
    seq = lambda n: pl.BlockSpec((1, 1, n), lambda i: (i // bps, 0, 0))
    vec = lambda n: pl.BlockSpec((1, n), lambda i: (0, 0))
    mat = lambda r, c: pl.BlockSpec((r, c), lambda i: (0, 0))
    outs = pl.pallas_call(
        functools.partial(_rwkv_prep_kernel, blocks_per_seq=bps, wb=wb, lw=lw, la=la),
        grid=(bt // tm,),
        in_specs=[row(w3), prev(w3), row(ns), prev(ns), seq(w3), seq(ns), vec(w3), vec(ns),
                  vec(wb), mat(LANES, wb), vec(wb), mat(LANES, wb), mat(g2.shape[0], wb), vec(wb), vec(wb)],
        out_specs=[row(wb)] * 7,
        out_shape=[jax.ShapeDtypeStruct((bt, wb), F32)] * 7,
        compiler_params=_params(1),
        name="rwkv_prep",
    )(zrkv, zrkv, zs, zs, shift_rkv.reshape(batch, 1, w3), shift_s.reshape(batch, 1, ns),
      mu_rkv.reshape(1, w3), mu_s.reshape(1, ns), w0.reshape(1, wb), pad_rows(w2), a0.reshape(1, wb),
      pad_rows(a2), g2.astype(BF16), kk_scale.reshape(1, wb), k_a.reshape(1, wb))
    return outs


def _rwkv_kernel(r_ref, ld_ref, k_ref, v_ref, kk_ref, a_ref, g_ref, rk_ref, gng_ref, gnb_ref, s0_ref,
                 y_ref, s1_ref, s_s, *, chunk, n_chunks, hs):
    tb = pl.program_id(2)

    @pl.when(tb == 0)
    def _():
        s_s[...] = s0_ref[0, 0]

    C = chunk
    R = 2 * C
    W = 2 * hs
    lane = lax.broadcasted_iota(jnp.int32, (R, W), 1)
    rowi = lax.broadcasted_iota(jnp.int32, (R, W), 0)
    headmask = (_div_pow2(rowi, C) == _div_pow2(lane, hs)).astype(F32)
    ri = lax.broadcasted_iota(jnp.int32, (R, R), 0)
    ci_ = lax.broadcasted_iota(jnp.int32, (R, R), 1)
    same = _div_pow2(ri, C) == _div_pow2(ci_, C)
    lower_incl = (same & (ci_ <= ri)).astype(F32)
    lower_strict = (same & (ci_ < ri)).astype(F32)
    eye_r = (ri == ci_).astype(F32)
    wi = lax.broadcasted_iota(jnp.int32, (W, W), 0)
    wj = lax.broadcasted_iota(jnp.int32, (W, W), 1)
    eye_w = (wi == wj).astype(F32)
    lane_row = lax.broadcasted_iota(jnp.int32, (1, W), 1)
    rk = rk_ref[...]
    gng = gng_ref[...]
    gnb = gnb_ref[...]
    chunks = range(n_chunks)
    rows = [slice(c * C, (c + 1) * C) for c in chunks]
    bf = lambda x: x.astype(BF16)

    def stack(ref):
        return [jnp.concatenate([ref[rows[c], :]] * 2, axis=0) * headmask for c in chunks]

    def split2(x):
        hi = bf(x)
        return hi, bf(x - hi.astype(F32))

    def split3(x):
        hi = bf(x)
        rest = x - hi.astype(F32)
        mid = bf(rest)
        return hi, mid, bf(rest - mid.astype(F32))

    rs, lds, ks, vs, kks, as_ = (stack(ref) for ref in (r_ref, ld_ref, k_ref, v_ref, kk_ref, a_ref))
    tri16 = bf(lower_incl)
    lam3 = [_dot(tri16, jnp.concatenate(split3(lds[c]), axis=1)) for c in chunks]
    lam = [lam3[c][:, :W] + (lam3[c][:, W:2 * W] + lam3[c][:, 2 * W:]) for c in chunks]
    e_pos = [jnp.exp(lam[c]) for c in chunks]
    e_neg = [jnp.exp(-lam[c]) for c in chunks]
    kkn = [kks[c] * lax.rsqrt(jnp.maximum(jnp.sum(kks[c] * kks[c], axis=-1, keepdims=True), 1e-24)) for c in chunks]
    kap = [bf(kkn[c] * jnp.exp(lam[c] - lds[c])) for c in chunks]
    bt = [bf(kkn[c] * as_[c] * e_neg[c]) for c in chunks]
    kt = [bf(ks[c] * e_neg[c]) for c in chunks]
    rho = [rs[c] * e_pos[c] for c in chunks]
    vb = [bf(vs[c]) for c in chunks]
    wide = R == LANES
    if wide:
        gram = [_dot_nt(jnp.concatenate([kap[c], bf(rho[c])], axis=0), jnp.concatenate([bt[c], kt[c]], axis=0))
                for c in chunks]
        a_kb = [gram[c][:R, :R] * lower_strict for c in chunks]
        a_kk = [bf(gram[c][:R, R:] * lower_strict) for c in chunks]
        a_rb = [bf(gram[c][R:, :R] * lower_incl) for c in chunks]
        a_rk = [bf(gram[c][R:, R:] * lower_incl) for c in chunks]
    else:
        a_kb = [_dot_nt(kap[c], bt[c]) * lower_strict for c in chunks]
        a_kk = [bf(_dot_nt(kap[c], kt[c]) * lower_strict) for c in chunks]
        a_rb = [bf(_dot_nt(bf(rho[c]), bt[c]) * lower_incl) for c in chunks]
        a_rk = [bf(_dot_nt(bf(rho[c]), kt[c]) * lower_incl) for c in chunks]
    pair = (_div_pow2(ri, 2) == _div_pow2(ci_, 2)).astype(F32)
    tinv = [eye_r - a_kb[c] * pair for c in chunks]
    bs = 4
    while bs <= C:
        lower_left = ((_div_pow2(ri, bs) == _div_pow2(ci_, bs)) & (_mod_pow2(ri, bs) >= bs // 2)
                      & (_mod_pow2(ci_, bs) < bs // 2)).astype(F32)
        t16 = [bf(tinv[c]) for c in chunks]
        left = [bf(_dot(t16[c], bf(a_kb[c] * lower_left))) for c in chunks]
        tinv = [tinv[c] - _dot(left[c], t16[c]) for c in chunks]
        bs *= 2
    t16 = [bf(tinv[c]) for c in chunks]
    av = [bf(_dot(a_kk[c], vb[c])) for c in chunks]
    if wide:
        ku = [_dot(t16[c], jnp.concatenate([kap[c], av[c]], axis=1)) for c in chunks]
        kap2 = [bf(ku[c][:, :W]) for c in chunks]
        u0 = [bf(ku[c][:, W:]) for c in chunks]
    else:
        kap2 = [bf(_dot(t16[c], kap[c])) for c in chunks]
        u0 = [bf(_dot(t16[c], av[c])) for c in chunks]
    gamma = [jnp.where(lane_row < hs, e_pos[c][C - 1:C, :], e_pos[c][R - 1:R, :]) for c in chunks]
    m_p = [(eye_w - _dot_tn(kap2[c], bt[c])) * gamma[c] for c in chunks]
    g_p = [_dot_tn(jnp.concatenate([vb[c], u0[c]], axis=0), jnp.concatenate([kt[c], -bt[c]], axis=0)) * gamma[c]
           for c in chunks]
    if wide:
        ru = [_dot(a_rb[c], jnp.concatenate([kap2[c], u0[c]], axis=1)) for c in chunks]
        rho2 = [rho[c] - ru[c][:, :W] for c in chunks]
        y0 = [_dot(a_rk[c], vb[c]) - ru[c][:, W:] for c in chunks]
    else:
        rho2 = [rho[c] - _dot(a_rb[c], kap2[c]) for c in chunks]
        y0 = [_dot(a_rk[c], vb[c]) - _dot(a_rb[c], u0[c]) for c in chunks]
    bonus_v = [jnp.sum(rs[c] * ks[c] * rk, axis=-1, keepdims=True) * vs[c] for c in chunks]

    rho2_hl = [jnp.concatenate(split2(rho2[c]), axis=0) for c in chunks]
    m_p_hl = [jnp.concatenate(split2(m_p[c]), axis=1) for c in chunks]
    s = s_s[...]
    for c in chunks:
        s_hl = jnp.concatenate(split2(s), axis=0)
        py = _dot_nt(rho2_hl[c], s_hl)
        ys = py[:R, :W] + (py[:R, W:] + py[R:, :W]) + y0[c]
        ps = _dot(s_hl, m_p_hl[c])
        s = ps[:W, :W] + (ps[:W, W:] + ps[W:, :W]) + g_p[c]
        mean = jnp.sum(ys, axis=-1, keepdims=True) / hs
        d = (ys - mean) * headmask
        var = jnp.sum(d * d, axis=-1, keepdims=True) / hs
        yn = (d * lax.rsqrt(var + GN_EPS_B) * gng + gnb) * headmask + bonus_v[c]
        y_ref[rows[c], :] = ((yn[:C] + yn[C:]) * g_ref[rows[c], :]).astype(y_ref.dtype)
    s_s[...] = s

    @pl.when(tb == pl.num_programs(2) - 1)
    def _():
        s1_ref[0, 0] = s


def _rwkv(r, ld, k, v, kk, a, g, r_k, gn_g, gn_b, s0, batch, out_dtype):
    bt, wb = r.shape
    t = bt // batch
    nh, hs = s0.shape[1], s0.shape[2]
    assert 2 * hs == LANES and nh % 2 == 0
    npair = nh // 2
    chunk = math.gcd(t, CHUNK_B)
    tblk = _tile(t, (512, 256, 128, 64))
    ntb = t // tblk
    s0p = s0.reshape(batch, npair, 2, hs, hs)
    zero = jnp.zeros_like(s0p[:, :, 0])
    s0bd = jnp.concatenate([jnp.concatenate([s0p[:, :, 0], zero], axis=-1),
                            jnp.concatenate([zero, s0p[:, :, 1]], axis=-1)], axis=-2)
    blk = pl.BlockSpec((tblk, LANES), lambda b, p, i: (b * ntb + i, p))
    vec = pl.BlockSpec((1, LANES), lambda b, p, i: (0, p))
    st = pl.BlockSpec((1, 1, LANES, LANES), lambda b, p, i: (b, p, 0, 0))
    y, s1bd = pl.pallas_call(
        functools.partial(_rwkv_kernel, chunk=chunk, n_chunks=tblk // chunk, hs=hs),
        grid=(batch, npair, ntb),
        in_specs=[blk] * 7 + [vec] * 3 + [st],
        out_specs=[blk, st],
        out_shape=[jax.ShapeDtypeStruct((bt, wb), out_dtype),
                   jax.ShapeDtypeStruct((batch, npair, LANES, LANES), F32)],
        scratch_shapes=[pltpu.VMEM((LANES, LANES), F32)],
        compiler_params=_params(3),
        name="rwkv",
    )(r, ld, k, v, kk, a, g, r_k.reshape(1, wb), gn_g.reshape(1, wb), gn_b.reshape(1, wb), s0bd)
    s1 = jnp.stack([s1bd[:, :, :hs, :hs], s1bd[:, :, hs:, hs:]], axis=2).reshape(batch, nh, hs, hs)
    return y, s1


def _lf_cum_kernel(z_ref, b_ref, lf_ref, g_ref, carry_s):
    @pl.when(pl.program_id(1) == 0)
    def _():
        carry_s[...] = jnp.zeros_like(carry_s)

    lf = _log_sigmoid(z_ref[...] + b_ref[...])
    rows = lf.shape[0]
    ti = lax.broadcasted_iota(jnp.int32, (rows, rows), 0)
    si = lax.broadcasted_iota(jnp.int32, (rows, rows), 1)
    cum = jnp.dot((si <= ti).astype(F32), lf, precision=HIGHEST, preferred_element_type=F32) + carry_s[...]
    lf_ref[...] = lf
    g_ref[...] = cum
    carry_s[...] = cum[rows - 1:rows, :]


def _lf_cum(zf, b_f, batch):
    bt = zf.shape[0]
    t = bt // batch
    tb = _tile(t, (256, 128, 64, 32, 16, 8))
    ntb = t // tb
    bpad = jnp.zeros((1, LANES), F32).at[0, :b_f.shape[0]].set(b_f)
    blk = pl.BlockSpec((tb, LANES), lambda b, i: (b * ntb + i, 0))
    return pl.pallas_call(
        _lf_cum_kernel,
        grid=(batch, ntb),
        in_specs=[blk, pl.BlockSpec((1, LANES), lambda b, i: (0, 0))],
        out_specs=[blk, blk],
        out_shape=[jax.ShapeDtypeStruct((bt, LANES), F32)] * 2,
        scratch_shapes=[pltpu.VMEM((1, LANES), F32)],
        compiler_params=_params(2),
        name="lf_cum",
    )(zf, bpad)


def _fox_kernel(q_ref, k_ref, v_ref, gq_ref, gk_ref, o_ref, kb_s, vb_s, *, bq, nq, scale):
    kb_s[...] = k_ref[...].astype(BF16)
    vb_s[...] = v_ref[...].astype(BF16)
    ti = lax.broadcasted_iota(jnp.int32, (bq, bq), 0)
    si = lax.broadcasted_iota(jnp.int32, (bq, bq), 1)

    for c in range(nq):
        past = c * bq
        rows = slice(past, past + bq)
        q = q_ref[rows, :].astype(BF16)
        gq = gq_ref[0, 0, rows, :]
        sd = _dot_nt(q, kb_s[rows, :]) * scale + (gq - gk_ref[0, 0, :, rows])
        sd = jnp.where(si <= ti, sd, NEG_INF)
        m = jnp.max(sd, axis=-1, keepdims=True)
        if past:
            sp = _dot_nt(q, kb_s[0:past, :]) * scale + (gq - gk_ref[0, 0, :, 0:past])
            m = jnp.maximum(m, jnp.max(sp, axis=-1, keepdims=True))
        pd = jnp.exp(sd - m)
        l = jnp.sum(pd, axis=-1, keepdims=True)
        acc = _dot(pd.astype(BF16), vb_s[rows, :])
        if past:
            pp = jnp.exp(sp - m)
            l = l + jnp.sum(pp, axis=-1, keepdims=True)
            acc = acc + _dot(pp.astype(BF16), vb_s[0:past, :])
        o_ref[rows, :] = (acc / l).astype(o_ref.dtype)


def _fox_prompt(q, k, v, g, batch, nh):
    bt, width = q.shape
    t = bt // batch
    dh = width // nh
    bq = _tile(t, (256, 128))
    nq = t // bq
    gq = jnp.transpose(g, (0, 2, 1)).reshape(batch, nh, t, 1)
    gk = jnp.transpose(g, (0, 2, 1)).reshape(batch, nh, 1, t)
    return pl.pallas_call(
        functools.partial(_fox_kernel, bq=bq, nq=nq, scale=dh ** -0.5),
        grid=(batch, nh),
        in_specs=[pl.BlockSpec((t, dh), lambda b, h: (b, h)),
                  pl.BlockSpec((t, dh), lambda b, h: (b, h)),
                  pl.BlockSpec((t, dh), lambda b, h: (b, h)),
                  pl.BlockSpec((1, 1, t, 1), lambda b, h: (b, h, 0, 0)),
                  pl.BlockSpec((1, 1, 1, t), lambda b, h: (b, h, 0, 0))],
        out_specs=pl.BlockSpec((t, dh), lambda b, h: (b, h)),
        out_shape=jax.ShapeDtypeStruct((bt, width), BF16),
        scratch_shapes=[pltpu.VMEM((t, dh), BF16), pltpu.VMEM((t, dh), BF16)],
        compiler_params=_params(2),
        name="fox_prompt",
    )(q, k, v, gq, gk)


def _gpast_kernel(pt_ref, *refs, n_pages):
    del pt_ref
    lf_refs, o_ref, carry_s = refs[:n_pages], refs[n_pages], refs[n_pages + 1]

    @pl.when(pl.program_id(1) == 0)
    def _():
        carry_s[...] = jnp.zeros_like(carry_s)

    rows = lf_refs[0].shape[2]
    ti = lax.broadcasted_iota(jnp.int32, (rows, rows), 0)
    si = lax.broadcasted_iota(jnp.int32, (rows, rows), 1)
    later_rows = (si > ti).astype(F32)
    carry = carry_s[...]
    for k in range(n_pages):
        lf = lf_refs[k][0, 0].astype(F32)
        later = jnp.dot(later_rows, lf, precision=HIGHEST, preferred_element_type=F32)
        o_ref[0, n_pages - 1 - k] = -(later + carry)
        carry = carry + jnp.sum(lf, axis=0, keepdims=True)
    carry_s[...] = carry


def _gpast(cache_logf, layer, page_table):
    batch, npg = page_table.shape
    _, _, page, nh = cache_logf.shape
    pps = _tile(npg, (8, 4, 2))
    nsteps = npg // pps

    def page_spec(k):
        return pl.BlockSpec((1, 1, page, nh), lambda b, p, pt: (layer, pt[b, npg - 1 - (p * pps + k)], 0, 0))

    grid_spec = pltpu.PrefetchScalarGridSpec(
        num_scalar_prefetch=1,
        grid=(batch, nsteps),
        in_specs=[page_spec(k) for k in range(pps)],
        out_specs=pl.BlockSpec((1, pps, page, nh), lambda b, p, pt: (b, nsteps - 1 - p, 0, 0)),
        scratch_shapes=[pltpu.VMEM((1, nh), F32)],
    )
    return pl.pallas_call(
        functools.partial(_gpast_kernel, n_pages=pps),
        grid_spec=grid_spec,
        out_shape=jax.ShapeDtypeStruct((batch, npg, page, nh), F32),
        compiler_params=_params(2),
        name="gpast",
    )(page_table, *([cache_logf] * pps))


HEAD_GROUP = 8


def _fox_dec_kernel(pt_ref, q_ref, kp_ref, vp_ref, gp_ref, kn_ref, vn_ref, gq_ref, gn_ref, o_ref,
                    base_s, m_s, l_s, acc_s, *, scale):
    del pt_ref
    p = pl.program_id(1)
    hg = HEAD_GROUP
    ng, nq, dh = q_ref.shape[1:]
    nk = kp_ref.shape[2] * hg

    @pl.when(p == 0)
    def _():
        qi = lax.broadcasted_iota(jnp.int32, (nq, nk), 0)
        ki = lax.broadcasted_iota(jnp.int32, (nq, nk), 1)
        same_head = _mod_pow2(qi, hg) == _mod_pow2(ki, hg)
        for g in range(ng):
            base_s[g] = jnp.where(same_head, gq_ref[0, g], NEG_INF)
        m_s[...] = jnp.full_like(m_s, NEG_INF)
        l_s[...] = jnp.zeros_like(l_s)
        acc_s[...] = jnp.zeros_like(acc_s)

    groups = range(ng)

    def update(s, v):
        m_old = [m_s[g] for g in groups]
        m_new = [jnp.maximum(m_old[g], jnp.max(s[g], axis=-1, keepdims=True)) for g in groups]
        pr = [jnp.exp(s[g] - m_new[g]) for g in groups]
        pv = [_dot(pr[g].astype(BF16), v[g]) for g in groups]
        for g in groups:
            alpha = jnp.exp(m_old[g] - m_new[g])
            l_s[g] = alpha * l_s[g] + jnp.sum(pr[g], axis=-1, keepdims=True)
            acc_s[g] = alpha * acc_s[g] + pv[g]
            m_s[g] = m_new[g]

    heads = [slice(g * hg, (g + 1) * hg) for g in groups]
    q = [q_ref[0, g].astype(BF16) for g in groups]
    k = [kp_ref[0, 0, :, heads[g], :].reshape(nk, dh).astype(BF16) for g in groups]
    v = [vp_ref[0, 0, :, heads[g], :].reshape(nk, dh).astype(BF16) for g in groups]
    update([_dot_nt(q[g], k[g]) * scale + (base_s[g] - gp_ref[0, 0, g]) for g in groups], v)

    @pl.when(p == pl.num_programs(1) - 1)
    def _():
        qi = lax.broadcasted_iota(jnp.int32, (nq, nq), 0)
        ki = lax.broadcasted_iota(jnp.int32, (nq, nq), 1)
        keep = (_mod_pow2(qi, hg) == _mod_pow2(ki, hg)) & (_div_pow2(ki, hg) <= _div_pow2(qi, hg))
        s = [_dot_nt(q[g], kn_ref[0, g].astype(BF16)) * scale + (gq_ref[0, g] - gn_ref[0, g]) for g in groups]
        update([jnp.where(keep, s[g], NEG_INF) for g in groups], [vn_ref[0, g].astype(BF16) for g in groups])
        for g in groups:
            o_ref[0, g] = (acc_s[g] / l_s[g]).astype(o_ref.dtype)


def _fox_decode(q, kn, vn, g_new, cache_k, cache_v, g_past, layer, page_table):
    batch, ts, nh, dh = q.shape
    npg = page_table.shape[1]
    page = cache_k.shape[2]
    hg = HEAD_GROUP
    assert nh % hg == 0
    ng = nh // hg
    rows = ts * hg

    def grouped(x):
        n = x.shape[-1]
        return jnp.transpose(x.reshape(batch, ts, ng, hg, n), (0, 2, 1, 3, 4)).reshape(batch, ng, rows, n)

    gq = grouped(g_new[..., None])
    gn = gq.reshape(batch, ng, 1, rows)
    gp = jnp.transpose(g_past.reshape(batch, npg, page, ng, hg), (0, 1, 3, 2, 4)).reshape(batch, npg, ng, 1, page * hg)
    new = lambda r, n: pl.BlockSpec((1, ng, r, n), lambda b, p, pt: (b, 0, 0, 0))
    pg = pl.BlockSpec((1, 1, page, nh, dh), lambda b, p, pt: (layer, pt[b, p], 0, 0, 0))
    grid_spec = pltpu.PrefetchScalarGridSpec(
        num_scalar_prefetch=1,
        grid=(batch, npg),
        in_specs=[new(rows, dh), pg, pg,
                  pl.BlockSpec((1, 1, ng, 1, page * hg), lambda b, p, pt: (b, p, 0, 0, 0)),
                  new(rows, dh), new(rows, dh), new(rows, 1), new(1, rows)],
        out_specs=new(rows, dh),
        scratch_shapes=[pltpu.VMEM((ng, rows, page * hg), F32), pltpu.VMEM((ng, rows, 1), F32),
                        pltpu.VMEM((ng, rows, 1), F32), pltpu.VMEM((ng, rows, dh), F32)],
    )
    out = pl.pallas_call(
        functools.partial(_fox_dec_kernel, scale=dh ** -0.5),
        grid_spec=grid_spec,
        out_shape=jax.ShapeDtypeStruct((batch, ng, rows, dh), F32),
        compiler_params=_params(2),
        name="fox_decode",
    )(page_table, grouped(q), cache_k, cache_v, gp, grouped(kn), grouped(vn), gq, gn)
    return jnp.transpose(out.reshape(batch, ng, ts, hg, dh), (0, 2, 1, 3, 4)).reshape(batch, ts, nh, dh)


def _even_mixers(z_a, zs, zrkv, batch, st, w, act):
    c0, n0, m0, s0, shift0 = st
    nh_a = c0.shape[1]
    t = z_a.shape[0] // batch
    gates = _mlstm_gates(zs, w["b_gate"], nh_a)
    h, c1, n1, m1 = _mlstm(z_a, gates, w["g_head"], c0, n0, m0, batch, act)
    wb3 = zrkv.shape[1]
    lw, la, lg = w["lw"], w["la"], w["lg"]
    sh_rkv = shift0[:, :wb3]
    sh_s = jnp.zeros((batch, 3 * LANES + lg), F32)
    sh_s = sh_s.at[:, LANES:LANES + lw].set(shift0[:, wb3:wb3 + lw])
    sh_s = sh_s.at[:, 2 * LANES:2 * LANES + la].set(shift0[:, wb3 + lw:wb3 + lw + la])
    sh_s = sh_s.at[:, 3 * LANES:].set(shift0[:, wb3 + lw + la:])
    r, ld, k, v, kk, a, g = _rwkv_prep(zrkv, zs, sh_rkv, sh_s, w["mu_rkv"], w["mu_s"], w["w0"], w["w2"],
                                       w["a0"], w["a2"], w["g2"], w["kk_scale"], w["k_a"], batch)
    y, s1 = _rwkv(r, ld, k, v, kk, a, g, w["r_k"], w["gn_g"], w["gn_b"], s0, batch, act)
    last_rkv = zrkv.reshape(batch, t, wb3)[:, -1]
    last_s = zs.reshape(batch, t, -1)[:, -1]
    shift1 = jnp.concatenate([last_rkv, last_s[:, LANES:LANES + lw], last_s[:, 2 * LANES:2 * LANES + la],
                              last_s[:, 3 * LANES:]], axis=-1)
    return h.astype(BF16), y.astype(BF16), (c1, n1, m1, s1, shift1)


def _post_block(xp, xs, mix_p, mix_s, pp, ps, w, layer, alpha):
    x1p, x1pb = _add_ln(xp, mix_p, w["ln_mix_g"], w["ln_mix_b"], alpha)
    x1s, x1sb = _add_ln(xs, mix_s, w["ln_mix_g"], w["ln_mix_b"], alpha)
    hp, hs = _swiglu(x1pb, x1sb, w["w_ffn_in"], layer)
    w_out = w["w_ffn_out"][layer].astype(BF16)
    x2p, x2pb = _add_ln(x1p, _mm(hp, w_out, F32), w["ln_ffn_g"], w["ln_ffn_b"], alpha)
    x2s, x2sb = _add_ln(x1s, _mm(hs, w_out, F32), w["ln_ffn_g"], w["ln_ffn_b"], alpha)
    return _ple(x2p, x2pb, pp, x2s, x2sb, ps, w["w_ple_gate"], w["w_ple"], layer)


def kernel(x_prompt, x_sample, cache_k, cache_v, cache_logf, state_mlstm_c, state_mlstm_n, state_mlstm_m, state_rwkv_wkv, state_rwkv_shift, page_table, p_prompt, p_sample, w_in_ab, b_gate_a, g_head_a, mu_b, w0_b, w2_b, a0_b, a2_b, g2_b, kk_scale_b, ka_b, rk_b, gn_g_b, gn_b_b, w_out_ab, w_in_c, b_f_c, w_out_c, ln_mix_g, ln_mix_b, ln_ffn_g, ln_ffn_b, w_ffn_in, w_ffn_out, w_ple, w_ple_gate):
    bp, tp, d = x_prompt.shape
    bs, ts, _ = x_sample.shape
    depth = ln_mix_g.shape[0]
    alpha = (2 * depth) ** 0.25
    nh_a, dh_a = state_mlstm_c.shape[2], state_mlstm_c.shape[3]
    wa = nh_a * dh_a
    nh_b, hs_b = state_rwkv_wkv.shape[2], state_rwkv_wkv.shape[3]
    wb = nh_b * hs_b
    lw, la, lg = w2_b.shape[1], a2_b.shape[1], g2_b.shape[1]
    nh_c, dh_c = cache_k.shape[3], cache_k.shape[4]
    n_in_a = 4 * wa + 2 * nh_a
    assert lw <= LANES and la <= LANES and 2 * nh_a <= LANES and nh_c <= LANES

    xp, xs = x_prompt.reshape(bp * tp, d), x_sample.reshape(bs * ts, d)
    xpb, xsb = xp.astype(BF16), xs.astype(BF16)
    pp = p_prompt.reshape(depth, bp * tp, -1).astype(BF16)
    ps = p_sample.reshape(depth, bs * ts, -1).astype(BF16)

    outs = {name: [] for name in ("kp", "vp", "lfp", "ks", "vs", "lfs", "cp", "np", "mp", "cs", "ns", "ms",
                                  "sp", "shp", "ss", "shs")}
    for i in range(depth):
        j = i // 2
        w = {"ln_mix_g": ln_mix_g[i], "ln_mix_b": ln_mix_b[i], "ln_ffn_g": ln_ffn_g[i], "ln_ffn_b": ln_ffn_b[i],
             "w_ffn_in": w_ffn_in, "w_ffn_out": w_ffn_out, "w_ple": w_ple, "w_ple_gate": w_ple_gate}
        if i % 2 == 0:
            w_in = w_in_ab[j]
            rk0 = n_in_a
            lo = rk0 + 3 * wb
            zc = lambda n: jnp.zeros((d, n), F32)
            w_small = jnp.concatenate([w_in[:, 4 * wa:n_in_a], zc(LANES - 2 * nh_a),
                                       w_in[:, lo:lo + lw], zc(LANES - lw),
                                       w_in[:, lo + lw:lo + lw + la], zc(LANES - la),
                                       w_in[:, lo + lw + la:]], axis=1)
            mu = mu_b[j]
            zv = lambda n: jnp.zeros((n,), F32)
            mu_s = jnp.concatenate([zv(LANES), mu[3 * wb:3 * wb + lw], zv(LANES - lw),
                                    mu[3 * wb + lw:3 * wb + lw + la], zv(LANES - la), mu[3 * wb + lw + la:]])
            w.update({"b_gate": b_gate_a[j], "g_head": g_head_a[j],
                      "mu_rkv": mu[:3 * wb], "mu_s": mu_s, "w0": w0_b[j], "w2": w2_b[j], "a0": a0_b[j],
                      "a2": a2_b[j], "g2": g2_b[j], "kk_scale": kk_scale_b[j], "k_a": ka_b[j],
                      "r_k": rk_b[j].reshape(-1), "gn_g": gn_g_b[j], "gn_b": gn_b_b[j],
                      "lw": lw, "la": la, "lg": lg})
            zero_state = (jnp.zeros((bp, nh_a, dh_a, dh_a), F32), jnp.zeros((bp, nh_a, dh_a), F32),
                          jnp.zeros((bp, nh_a), F32), jnp.zeros((bp, nh_b, hs_b, hs_b), F32),
                          jnp.zeros((bp, mu.shape[0]), F32))
            sample_state = (state_mlstm_c[j], state_mlstm_n[j], state_mlstm_m[j], state_rwkv_wkv[j],
                            state_rwkv_shift[j])
            za_p, za_s = _proj(xpb, xsb, w_in_ab, j, 0, 4 * wa, BF16)
            zs_p, zs_s = _proj(xpb, xsb, w_small[None], 0, 0, w_small.shape[1], F32)
            zr_p, zr_s = _proj(xpb, xsb, w_in[None, :, rk0:lo], 0, 0, 3 * wb, F32)
            hp, yp, (cp, np_, mp, sp, shp) = _even_mixers(za_p, zs_p, zr_p, bp, zero_state, w, BF16)
            hs, ys, (cs, ns, ms, ss, shs) = _even_mixers(za_s, zs_s, zr_s, bs, sample_state, w, F32)
            mix_p, mix_s = _proj2(hp, yp, hs, ys, w_out_ab, j)
            for name, val in (("cp", cp), ("np", np_), ("mp", mp), ("sp", sp), ("shp", shp),
                              ("cs", cs), ("ns", ns), ("ms", ms), ("ss", ss), ("shs", shs)):
                outs[name].append(val)
        else:
            w_f = jnp.concatenate([w_in_c[j][:, 3 * d:], jnp.zeros((d, LANES - nh_c), F32)], axis=1)
            qp, qs = _proj(xpb, xsb, w_in_c, j, 0, d, BF16)
            kp, kn = _proj(xpb, xsb, w_in_c, j, d, d, F32)
            vp, vn = _proj(xpb, xsb, w_in_c, j, 2 * d, d, F32)
            zfp, zfs = _proj(xpb, xsb, w_f[None], 0, 0, LANES, F32)
            lfp, gp = _lf_cum(zfp, b_f_c[j], bp)
            lfp = lfp[:, :nh_c].reshape(bp, tp, nh_c)
            att_p = _fox_prompt(qp, kp, vp, gp[:, :nh_c].reshape(bp, tp, nh_c), bp, nh_c)
            lfs, g_new = _lf_cum(zfs, b_f_c[j], bs)
            lfs = lfs[:, :nh_c].reshape(bs, ts, nh_c)
            g_past = _gpast(cache_logf, j, page_table)
            heads = lambda x: x.reshape(bs, ts, nh_c, dh_c)
            att_s = _fox_decode(heads(qs), heads(kn), heads(vn), g_new[:, :nh_c].reshape(bs, ts, nh_c),
                                cache_k, cache_v, g_past, j, page_table)
            mix_p, mix_s = _proj(att_p, att_s.reshape(bs * ts, d).astype(BF16), w_out_c, j, 0, d, F32)
            for name, val in (("kp", kp.reshape(bp, tp, nh_c, dh_c)), ("vp", vp.reshape(bp, tp, nh_c, dh_c)),
                              ("lfp", lfp), ("ks", kn.reshape(bs, ts, nh_c, dh_c)),
                              ("vs", vn.reshape(bs, ts, nh_c, dh_c)), ("lfs", lfs)):
                outs[name].append(val)
        xp, xpb, xs, xsb = _post_block(xp, xs, mix_p, mix_s, pp[i], ps[i], w, i, alpha)
    stk = lambda name: jnp.stack(outs[name])
    return (xp.reshape(bp, tp, d), xs.reshape(bs, ts, d),
            stk("kp"), stk("vp"), stk("lfp"), stk("ks"), stk("vs"), stk("lfs"),
            stk("cp"), stk("np"), stk("mp"), stk("cs"), stk("ns"), stk("ms"),
            stk("sp"), stk("shp"), stk("ss"), stk("shs"))
```

```python
import functools
import math

import jax
import jax.numpy as jnp
from jax import lax
from jax.experimental import pallas as pl
from jax.experimental.pallas import tpu as pltpu

F32 = jnp.float32
BF16 = jnp.bfloat16
HIGHEST = lax.Precision.HIGHEST

LANES = 128
VMEM_LIMIT_BYTES = 56 * 1024 * 1024
PANEL_DOUBLE_BUFFER_BYTES = VMEM_LIMIT_BYTES // 4

CHUNK_A = 64
GATE_CAP = 15.0
HEAD_EPS_A = 1e-6
CHUNK_B = 64
GN_EPS_B = 64e-5
LN_EPS = 1e-5
NEG_INF = float("-inf")


def _tile(dim, prefs):
    for t in prefs:
        if t <= dim and dim % t == 0:
            return t
    return dim


def _params(n_axes):
    return pltpu.CompilerParams(dimension_semantics=("arbitrary",) * n_axes,
                                vmem_limit_bytes=VMEM_LIMIT_BYTES)


def _dot(a, b):
    return jnp.dot(a, b, preferred_element_type=F32)


def _dot_nt(a, b):
    return lax.dot_general(a, b, (((1,), (1,)), ((), ())), preferred_element_type=F32)


def _dot_tn(a, b):
    return lax.dot_general(a, b, (((0,), (0,)), ((), ())), preferred_element_type=F32)


def _div_pow2(x, n):
    assert n & (n - 1) == 0
    return lax.shift_right_logical(x, n.bit_length() - 1)


def _mod_pow2(x, n):
    assert n & (n - 1) == 0
    return jnp.bitwise_and(x, n - 1)


def _log_sigmoid(x):
    return jnp.minimum(x, 0.0) - jnp.log1p(jnp.exp(-jnp.abs(x)))


def _softplus(x):
    return jnp.maximum(x, 0.0) + jnp.log1p(jnp.exp(-jnp.abs(x)))


def _mm_kernel(x_ref, w_ref, o_ref):
    o_ref[...] = _dot(x_ref[...], w_ref[...]).astype(o_ref.dtype)


def _mm(x, w, layer, out_dtype):
    m, k = x.shape
    n = w.shape[2]
    tm = _tile(m, (1024, 512, 256, 128, 64, 32, 16, 8))
    panel_bytes = tm * k * x.dtype.itemsize
    big_panel = panel_bytes > PANEL_DOUBLE_BUFFER_BYTES
    tn = _tile(n, (256, 128) if big_panel else (512, 384, 256, 128))
    x_spec = pl.BlockSpec((tm, k), lambda i, j: (i, 0), **({"pipeline_mode": pl.Buffered(1)} if big_panel else {}))
    return pl.pallas_call(
        _mm_kernel,
        grid=(m // tm, n // tn),
        in_specs=[x_spec,
                  pl.BlockSpec((None, k, tn), lambda i, j: (layer, 0, j))],
        out_specs=pl.BlockSpec((tm, tn), lambda i, j: (i, j)),
        out_shape=jax.ShapeDtypeStruct((m, n), out_dtype),
        compiler_params=_params(2),
        name="mm",
    )(x, w)


ROW_TILES = (1024, 512, 256, 128, 64, 32, 16, 8)
COL_TILES = (640, 512, 384, 256, 128)


def _w_spec(k, tn, layer, row_blk, col_blk0):
    return pl.BlockSpec((None, k, tn), lambda j, i: (layer, row_blk, col_blk0 + j))


def _first_row_block():
    return pl.program_id(1) == 0


def _proj_kernel(x_ref, xs_ref, w_ref, *refs, transposed):
    o_ref, os_ref, wb_s = refs[-3:]
    mul = _dot_nt if transposed else _dot

    @pl.when(_first_row_block())
    def _():
        wb_s[...] = w_ref[...].astype(BF16)
        os_ref[...] = mul(xs_ref[...], wb_s[...]).astype(os_ref.dtype)

    o_ref[...] = mul(x_ref[...], wb_s[...]).astype(o_ref.dtype)


def _proj(xp, xs, w, layer, col0, n, out_p, out_s=F32, stacked=None, transposed=False):
    m, k = xp.shape
    ms = xs.shape[0]
    tm = _tile(m, ROW_TILES)
    tn = _tile(n, COL_TILES)
    assert w.shape[2 if transposed else 1] == k and col0 % tn == 0
    if transposed:
        w_spec = pl.BlockSpec((None, tn, k), lambda j, i: (layer, col0 // tn + j, 0))
    else:
        w_spec = _w_spec(k, tn, layer, 0, col0 // tn)
    operands = [xp, xs, w]
    in_specs = [pl.BlockSpec((tm, k), lambda j, i: (i, 0)),
                pl.BlockSpec((ms, k), lambda j, i: (0, 0)),
                w_spec]
    aliases = {}
    if stacked is None:
        out_p_shape = jax.ShapeDtypeStruct((m, n), out_p)
        out_p_spec = pl.BlockSpec((tm, tn), lambda j, i: (i, j))
    else:
        buffer, slot, slots = stacked
        out_p_shape = jax.ShapeDtypeStruct((slots, m, n), out_p)
        out_p_spec = pl.BlockSpec((None, tm, tn), lambda j, i: (slot, i, j))
        if buffer is not None:
            operands.append(buffer)
            in_specs.append(pl.BlockSpec(memory_space=pl.ANY))
            aliases = {3: 0}
    return pl.pallas_call(
        functools.partial(_proj_kernel, transposed=transposed),
        grid=(n // tn, m // tm),
        in_specs=in_specs,
        out_specs=[out_p_spec, pl.BlockSpec((ms, tn), lambda j, i: (0, j))],
        out_shape=[out_p_shape, jax.ShapeDtypeStruct((ms, n), out_s)],
        scratch_shapes=[pltpu.VMEM((tn, k) if transposed else (k, tn), BF16)],
        input_output_aliases=aliases,
        compiler_params=_params(2),
        name="proj",
    )(*operands)


def _proj2_kernel(x1_ref, x2_ref, x1s_ref, x2s_ref, w1_ref, w2_ref, o_ref, os_ref, w1b_s, w2b_s):
    @pl.when(_first_row_block())
    def _():
        w1b_s[...] = w1_ref[...].astype(BF16)
        w2b_s[...] = w2_ref[...].astype(BF16)
        os_ref[...] = _dot(x1s_ref[...], w1b_s[...]) + _dot(x2s_ref[...], w2b_s[...])

    o_ref[...] = _dot(x1_ref[...], w1b_s[...]) + _dot(x2_ref[...], w2b_s[...])


def _proj2(x1, x2, x1s, x2s, w, layer):
    m, k = x1.shape
    ms = x1s.shape[0]
    n = w.shape[2]
    assert x2.shape == (m, k) and w.shape[1] == 2 * k
    tm = _tile(m, ROW_TILES)
    tn = _tile(n, COL_TILES)
    rows = pl.BlockSpec((tm, k), lambda j, i: (i, 0))
    rows_s = pl.BlockSpec((ms, k), lambda j, i: (0, 0))
    return pl.pallas_call(
        _proj2_kernel,
        grid=(n // tn, m // tm),
        in_specs=[rows, rows, rows_s, rows_s, _w_spec(k, tn, layer, 0, 0), _w_spec(k, tn, layer, 1, 0)],
        out_specs=[pl.BlockSpec((tm, tn), lambda j, i: (i, j)),
                   pl.BlockSpec((ms, tn), lambda j, i: (0, j))],
        out_shape=[jax.ShapeDtypeStruct((m, n), F32), jax.ShapeDtypeStruct((ms, n), F32)],
        scratch_shapes=[pltpu.VMEM((k, tn), BF16), pltpu.VMEM((k, tn), BF16)],
        compiler_params=_params(2),
        name="proj2",
    )(x1, x2, x1s, x2s, w, w)


def _add_ln_kernel(x_ref, y_ref, g_ref, b_ref, o_ref, ob_ref, *, alpha):
    v = alpha * x_ref[...] + y_ref[...]
    mu = jnp.mean(v, axis=-1, keepdims=True)
    d = v - mu
    var = jnp.mean(d * d, axis=-1, keepdims=True)
    out = d * lax.rsqrt(var + LN_EPS) * g_ref[...] + b_ref[...]
    o_ref[...] = out
    ob_ref[...] = out.astype(BF16)


def _add_ln(x, y, g, b, alpha):
    m, d = x.shape
    tm = _tile(m, (256, 128, 64, 32, 16))
    row = pl.BlockSpec((tm, d), lambda i: (i, 0))
    vec = pl.BlockSpec((1, d), lambda i: (0, 0))
    return pl.pallas_call(
        functools.partial(_add_ln_kernel, alpha=alpha),
        grid=(m // tm,),
        in_specs=[row, row, vec, vec],
        out_specs=[row, row],
        out_shape=[jax.ShapeDtypeStruct((m, d), F32), jax.ShapeDtypeStruct((m, d), BF16)],
        compiler_params=_params(1),
        name="add_ln",
    )(x, y, g.reshape(1, d), b.reshape(1, d))


def _swiglu_kernel(x_ref, xs_ref, wg_ref, wu_ref, o_ref, os_ref, wgb_s, wub_s):
    def act(x):
        gate = _dot(x, wgb_s[...])
        return (gate * jax.nn.sigmoid(gate) * _dot(x, wub_s[...])).astype(BF16)

    @pl.when(_first_row_block())
    def _():
        wgb_s[...] = wg_ref[...].astype(BF16)
        wub_s[...] = wu_ref[...].astype(BF16)
        os_ref[...] = act(xs_ref[...])

    o_ref[...] = act(x_ref[...])


def _swiglu(xp, xs, w_in, layer):
    m, d = xp.shape
    ms = xs.shape[0]
    f = w_in.shape[2] // 2
    tm = _tile(m, ROW_TILES)
    tf = _tile(f, (256, 128))
    nf = f // tf
    return pl.pallas_call(
        _swiglu_kernel,
        grid=(nf, m // tm),
        in_specs=[pl.BlockSpec((tm, d), lambda j, i: (i, 0)),
                  pl.BlockSpec((ms, d), lambda j, i: (0, 0)),
                  _w_spec(d, tf, layer, 0, 0), _w_spec(d, tf, layer, 0, nf)],
        out_specs=[pl.BlockSpec((tm, tf), lambda j, i: (i, j)),
                   pl.BlockSpec((ms, tf), lambda j, i: (0, j))],
        out_shape=[jax.ShapeDtypeStruct((m, f), BF16), jax.ShapeDtypeStruct((ms, f), BF16)],
        scratch_shapes=[pltpu.VMEM((d, tf), BF16), pltpu.VMEM((d, tf), BF16)],
        compiler_params=_params(2),
        name="swiglu",
    )(xp, xs, w_in, w_in)


def _ple_kernel(xb_ref, xsb_ref, pb_ref, psb_ref, x_ref, xs_ref, wg_ref, wp_ref,
                o_ref, ob_ref, os_ref, osb_ref, wgb_s, wpb_s):
    def gated(x, xb, pb):
        return x + jax.nn.sigmoid(_dot(xb, wgb_s[...])) * _dot(pb, wpb_s[...])

    @pl.when(_first_row_block())
    def _():
        wgb_s[...] = wg_ref[...].astype(BF16)
        wpb_s[...] = wp_ref[...].astype(BF16)
        out_s = gated(xs_ref[...], xsb_ref[...], psb_ref[...])
        os_ref[...] = out_s
        osb_ref[...] = out_s.astype(BF16)

    out = gated(x_ref[...], xb_ref[...], pb_ref[...])
    o_ref[...] = out
    ob_ref[...] = out.astype(BF16)


def _ple(xp, xpb, pp, xs, xsb, ps, w_gate, w_ple, layer):
    m, d = xp.shape
    ms = xs.shape[0]
    dp = pp.shape[1]
    tm = _tile(m, ROW_TILES)
    tn = _tile(d, (512, 256, 128))
    tile = pl.BlockSpec((tm, tn), lambda j, i: (i, j))
    tile_s = pl.BlockSpec((ms, tn), lambda j, i: (0, j))
    return pl.pallas_call(
        _ple_kernel,
        grid=(d // tn, m // tm),
        in_specs=[pl.BlockSpec((tm, d), lambda j, i: (i, 0)),
                  pl.BlockSpec((ms, d), lambda j, i: (0, 0)),
                  pl.BlockSpec((tm, dp), lambda j, i: (i, 0)),
                  pl.BlockSpec((ms, dp), lambda j, i: (0, 0)),
                  tile, tile_s,
                  _w_spec(d, tn, layer, 0, 0), _w_spec(dp, tn, layer, 0, 0)],
        out_specs=[tile, tile, tile_s, tile_s],
        out_shape=[jax.ShapeDtypeStruct((m, d), F32), jax.ShapeDtypeStruct((m, d), BF16),
                   jax.ShapeDtypeStruct((ms, d), F32), jax.ShapeDtypeStruct((ms, d), BF16)],
        scratch_shapes=[pltpu.VMEM((d, tn), BF16), pltpu.VMEM((dp, tn), BF16)],
        compiler_params=_params(2),
        name="ple",
    )(xpb, xsb, pp, ps, xp, xs, w_gate, w_ple)


def _mlstm_gate_kernel(z_ref, b_ref, o_ref, *, nh):
    g = GATE_CAP * jnp.tanh((z_ref[...] + b_ref[...]) / GATE_CAP)
    lane = lax.broadcasted_iota(jnp.int32, g.shape, 1)
    o_ref[...] = jnp.where(lane < nh, g, _log_sigmoid(g))


def _mlstm_gates(zs, b_gate, nh):
    m = zs.shape[0]
    tm = _tile(m, (1024, 512, 256, 128, 64, 32, 16, 8))
    bpad = jnp.zeros((1, LANES), F32).at[0, :2 * nh].set(b_gate)
    return pl.pallas_call(
        functools.partial(_mlstm_gate_kernel, nh=nh),
        grid=(m // tm,),
        in_specs=[pl.BlockSpec((tm, LANES), lambda i: (i, 0)),
                  pl.BlockSpec((1, LANES), lambda i: (0, 0))],
        out_specs=pl.BlockSpec((tm, LANES), lambda i: (i, 0)),
        out_shape=jax.ShapeDtypeStruct((m, LANES), F32),
        compiler_params=_params(1),
        name="mlstm_gates",
    )(zs, bpad)


def _mlstm_kernel(q_ref, k_ref, v_ref, o_ref, gc_ref, gr_ref, gh_ref, c0_ref, n0_ref, m0_ref,
                  h_ref, c1_ref, n1_ref, m1_ref, c_s, n_s, m_s, *, chunk, n_chunks, scale):
    tb = pl.program_id(2)

    @pl.when(tb == 0)
    def _():
        c_s[...] = c0_ref[0, 0]
        n_s[...] = n0_ref[0, 0]
        m_s[...] = m0_ref[0, 0]

    L = chunk
    ti = lax.broadcasted_iota(jnp.int32, (L, L), 0)
    si = lax.broadcasted_iota(jnp.int32, (L, L), 1)
    causal = si <= ti
    gc = gc_ref[0, 0]
    gr = gr_ref[0, 0]
    c = c_s[...]
    n = n_s[...]
    m = m_s[...]
    for ci in range(n_chunks):
        sl = slice(ci * L, (ci + 1) * L)
        q = q_ref[sl, :].astype(BF16)
        ks = k_ref[sl, :].astype(F32) * scale
        v = v_ref[sl, :].astype(BF16)
        ig_col, lf_col = gc[sl, 0:1], gc[sl, 1:2]
        ig_row, lf_row = gr[0:1, sl], gr[1:2, sl]
        bcum_col = jnp.sum(jnp.where(causal, lf_row, 0.0), axis=1, keepdims=True)
        bcum_row = jnp.sum(jnp.where(ti <= si, lf_col, 0.0), axis=0, keepdims=True)
        inter = bcum_col + m
        logd = jnp.where(causal, bcum_col - bcum_row + ig_row, NEG_INF)
        m_t = jnp.maximum(inter, jnp.max(logd, axis=1, keepdims=True))
        w_carry = jnp.exp(inter - m_t)
        s = _dot_nt(q, ks.astype(BF16)) * jnp.exp(logd - m_t)
        num = _dot(s.astype(BF16), v) + w_carry * _dot(q, c.astype(BF16))
        qn = jnp.sum(q.astype(F32) * n, axis=1, keepdims=True)
        den = jnp.sum(s, axis=1, keepdims=True) + w_carry * qn
        h = num / jnp.maximum(jnp.abs(den), jnp.exp(-m_t))
        m_new = m_t[L - 1:L, :]
        w_c = jnp.exp(inter[L - 1:L, :] - m_new)
        w_rows = jnp.exp(bcum_col[L - 1:L, :] - bcum_col + ig_col - m_new)
        kw = w_rows * ks
        c = w_c * c + _dot_tn(kw.astype(BF16), v)
        n = w_c * n + jnp.sum(kw, axis=0, keepdims=True)
        m = m_new
        mu = jnp.mean(h, axis=-1, keepdims=True)
        d = h - mu
        var = jnp.mean(d * d, axis=-1, keepdims=True)
        hn = d * lax.rsqrt(var + HEAD_EPS_A) * gh_ref[0] * jax.nn.sigmoid(o_ref[sl, :].astype(F32))
        h_ref[sl, :] = hn.astype(h_ref.dtype)
    c_s[...] = c
    n_s[...] = n
    m_s[...] = m

    @pl.when(tb == pl.num_programs(2) - 1)
    def _():
        c1_ref[0, 0] = c
        n1_ref[0, 0] = n
        m1_ref[0, 0] = m


def _mlstm(z, gates, g_head, c0, n0, m0, batch, out_dtype):
    bt = z.shape[0]
    t = bt // batch
    nh, dh = c0.shape[1], c0.shape[2]
    chunk = math.gcd(t, CHUNK_A)
    tblk = _tile(t, (256, 128, 64))
    ntb = t // tblk
    g = gates[:, :2 * nh].reshape(batch, t, 2, nh)
    g_col = jnp.transpose(g, (0, 3, 1, 2))
    g_row = jnp.transpose(g, (0, 3, 2, 1))

    def col(off):
        return pl.BlockSpec((tblk, dh), lambda b, h, i: (b * ntb + i, off + h))

    state = lambda r, c: pl.BlockSpec((1, 1, r, c), lambda b, h, i: (b, h, 0, 0))
    h, c1, n1, m1 = pl.pallas_call(
        functools.partial(_mlstm_kernel, chunk=chunk, n_chunks=tblk // chunk, scale=dh ** -0.5),
        grid=(batch, nh, ntb),
        in_specs=[col(0), col(nh), col(2 * nh), col(3 * nh),
                  pl.BlockSpec((1, 1, tblk, 2), lambda b, h, i: (b, h, i, 0)),
                  pl.BlockSpec((1, 1, 2, tblk), lambda b, h, i: (b, h, 0, i)),
                  pl.BlockSpec((1, 1, dh), lambda b, h, i: (h, 0, 0)),
                  state(dh, dh), state(1, dh), state(1, 1)],
        out_specs=[col(0), state(dh, dh), state(1, dh), state(1, 1)],
        out_shape=[jax.ShapeDtypeStruct((bt, nh * dh), out_dtype),
                   jax.ShapeDtypeStruct((batch, nh, dh, dh), F32),
                   jax.ShapeDtypeStruct((batch, nh, 1, dh), F32),
                   jax.ShapeDtypeStruct((batch, nh, 1, 1), F32)],
        scratch_shapes=[pltpu.VMEM((dh, dh), F32), pltpu.VMEM((1, dh), F32), pltpu.VMEM((1, 1), F32)],
        compiler_params=_params(3),
        name="mlstm",
    )(z, z, z, z, g_col, g_row, g_head.reshape(nh, 1, dh), c0,
      n0.reshape(batch, nh, 1, dh), m0.reshape(batch, nh, 1, 1))
    return h, c1, n1.reshape(batch, nh, dh), m1.reshape(batch, nh)


def _rwkv_prep_kernel(z_ref, zp_ref, sh_ref, mu_ref, w0_ref, w2_ref, a0_ref, a2_ref, g2_ref, kks_ref, ka_ref,
                      r_ref, ld_ref, k_ref, v_ref, kk_ref, a_ref, g_ref, *, blocks_per_seq, off, wb, lw, la, lg):
    i = pl.program_id(0)
    first = (i % blocks_per_seq) == 0
    cur = z_ref[...]
    prev_row = jnp.where(first, sh_ref[0], zp_ref[7:8, :])
    rolled = pltpu.roll(cur, 1, 0) if cur.shape[0] > 1 else cur
    ridx = lax.broadcasted_iota(jnp.int32, cur.shape, 0)
    zm = cur + mu_ref[...] * (jnp.where(ridx == 0, prev_row, rolled) - cur)
    r, k, v = zm[:, off:off + wb], zm[:, off + wb:off + 2 * wb], zm[:, off + 2 * wb:off + 3 * wb]
    lo = off + 3 * wb
    wl = zm[:, lo:lo + LANES]
    al = zm[:, lo + lw:lo + lw + LANES]
    gl = zm[:, lo + lw + la:lo + lw + la + lg]
    w_pre = w0_ref[...] + _dot(jnp.tanh(wl).astype(BF16), w2_ref[...])
    ld_ref[...] = -jnp.exp(-_softplus(-w_pre) - 0.5)
    a = jax.nn.sigmoid(a0_ref[...] + _dot(al.astype(BF16), a2_ref[...]))
    g_ref[...] = _dot(jax.nn.sigmoid(gl).astype(BF16), g2_ref[...])
    r_ref[...] = r
    v_ref[...] = v
    a_ref[...] = a
    kk_ref[...] = k * kks_ref[...]
    k_ref[...] = k * (1.0 + (a - 1.0) * ka_ref[...])


def _rwkv_prep(z, off, shift0, mu, w0, w2, a0, a2, g2, kk_scale, k_a, batch):
    bt, n = z.shape
    wb = w0.shape[0]
    t = bt // batch
    lw, la, lg = w2.shape[0], a2.shape[0], g2.shape[0]
    cols = 3 * wb + lw + la + lg
    assert off + cols <= n and off + 3 * wb + lw + la + LANES <= n and lw <= LANES and la <= LANES
    tm = _tile(t, (128, 64, 32, 16, 8))
    bps = t // tm
    pad_rows = lambda w: jnp.zeros((LANES, wb), BF16).at[:w.shape[0]].set(w.astype(BF16))
    placed = lambda x: jnp.zeros(x.shape[:-1] + (n,), F32).at[..., off:off + cols].set(x)
    row = lambda c: pl.BlockSpec((tm, c), lambda i: (i, 0))
    vec = lambda c: pl.BlockSpec((1, c), lambda i: (0, 0))
    mat = lambda r, c: pl.BlockSpec((r, c), lambda i: (0, 0))
    outs = pl.pallas_call(
        functools.partial(_rwkv_prep_kernel, blocks_per_seq=bps, off=off, wb=wb, lw=lw, la=la, lg=lg),
        grid=(bt // tm,),
        in_specs=[row(n),
                  pl.BlockSpec((8, n), lambda i: (jnp.maximum(i * (tm // 8) - 1, 0), 0)),
                  pl.BlockSpec((1, 1, n), lambda i: (i // bps, 0, 0)),
                  vec(n), vec(wb), mat(LANES, wb), vec(wb), mat(LANES, wb), mat(lg, wb), vec(wb), vec(wb)],
        out_specs=[row(wb)] * 7,
        out_shape=[jax.ShapeDtypeStruct((bt, wb), F32)] * 7,
        compiler_params=_params(1),
        name="rwkv_prep",
    )(z, z, placed(shift0).reshape(batch, 1, n), placed(mu).reshape(1, n), w0.reshape(1, wb), pad_rows(w2),
      a0.reshape(1, wb), pad_rows(a2), g2.astype(BF16), kk_scale.reshape(1, wb), k_a.reshape(1, wb))
    return outs


def _rwkv_kernel(r_ref, ld_ref, k_ref, v_ref, kk_ref, a_ref, g_ref, rk_ref, gng_ref, gnb_ref, s0_ref,
                 y_ref, s1_ref, s_s, *, chunk, n_chunks, hs):
    tb = pl.program_id(2)

    @pl.when(tb == 0)
    def _():
        s_s[...] = s0_ref[0, 0]

    C = chunk
    R = 2 * C
    W = 2 * hs
    lane = lax.broadcasted_iota(jnp.int32, (R, W), 1)
    rowi = lax.broadcasted_iota(jnp.int32, (R, W), 0)
    headmask = (_div_pow2(rowi, C) == _div_pow2(lane, hs)).astype(F32)
    ri = lax.broadcasted_iota(jnp.int32, (R, R), 0)
    ci_ = lax.broadcasted_iota(jnp.int32, (R, R), 1)
    same = _div_pow2(ri, C) == _div_pow2(ci_, C)
    lower_incl = (same & (ci_ <= ri)).astype(F32)
    lower_strict = (same & (ci_ < ri)).astype(F32)
    eye_r = (ri == ci_).astype(F32)
    wi = lax.broadcasted_iota(jnp.int32, (W, W), 0)
    wj = lax.broadcasted_iota(jnp.int32, (W, W), 1)
    eye_w = (wi == wj).astype(F32)
    lane_row = lax.broadcasted_iota(jnp.int32, (1, W), 1)
    rk = rk_ref[...]
    gng = gng_ref[...]
    gnb = gnb_ref[...]
    chunks = range(n_chunks)
    rows = [slice(c * C, (c + 1) * C) for c in chunks]
    bf = lambda x: x.astype(BF16)

    def stack(ref):
        return [jnp.concatenate([ref[rows[c], :]] * 2, axis=0) * headmask for c in chunks]

    def split2(x):
        hi = bf(x)
        return hi, bf(x - hi.astype(F32))

    def split3(x):
        hi = bf(x)
        rest = x - hi.astype(F32)
        mid = bf(rest)
        return hi, mid, bf(rest - mid.astype(F32))

    rs, lds, ks, vs, kks, as_ = (stack(ref) for ref in (r_ref, ld_ref, k_ref, v_ref, kk_ref, a_ref))
    tri16 = bf(lower_incl)
    lam3 = [_dot(tri16, jnp.concatenate(split3(lds[c]), axis=1)) for c in chunks]
    lam = [lam3[c][:, :W] + (lam3[c][:, W:2 * W] + lam3[c][:, 2 * W:]) for c in chunks]
    e_pos = [jnp.exp(lam[c]) for c in chunks]
    e_neg = [jnp.exp(-lam[c]) for c in chunks]
    kkn = [kks[c] * lax.rsqrt(jnp.maximum(jnp.sum(kks[c] * kks[c], axis=-1, keepdims=True), 1e-24)) for c in chunks]
    kap = [bf(kkn[c] * jnp.exp(lam[c] - lds[c])) for c in chunks]
    bt = [bf(kkn[c] * as_[c] * e_neg[c]) for c in chunks]
    kt = [bf(ks[c] * e_neg[c]) for c in chunks]
    rho = [rs[c] * e_pos[c] for c in chunks]
    vb = [bf(vs[c]) for c in chunks]
    wide = R == LANES
    if wide:
        gram = [_dot_nt(jnp.concatenate([kap[c], bf(rho[c])], axis=0), jnp.concatenate([bt[c], kt[c]], axis=0))
                for c in chunks]
        a_kb = [gram[c][:R, :R] * lower_strict for c in chunks]
        a_kk = [bf(gram[c][:R, R:] * lower_strict) for c in chunks]
        a_rb = [bf(gram[c][R:, :R] * lower_incl) for c in chunks]
        a_rk = [bf(gram[c][R:, R:] * lower_incl) for c in chunks]
    else:
        a_kb = [_dot_nt(kap[c], bt[c]) * lower_strict for c in chunks]
        a_kk = [bf(_dot_nt(kap[c], kt[c]) * lower_strict) for c in chunks]
        a_rb = [bf(_dot_nt(bf(rho[c]), bt[c]) * lower_incl) for c in chunks]
        a_rk = [bf(_dot_nt(bf(rho[c]), kt[c]) * lower_incl) for c in chunks]
    pair = (_div_pow2(ri, 2) == _div_pow2(ci_, 2)).astype(F32)
    tinv = [eye_r - a_kb[c] * pair for c in chunks]
    bs = 4
    while bs <= C:
        lower_left = ((_div_pow2(ri, bs) == _div_pow2(ci_, bs)) & (_mod_pow2(ri, bs) >= bs // 2)
                      & (_mod_pow2(ci_, bs) < bs // 2)).astype(F32)
        t16 = [bf(tinv[c]) for c in chunks]
        left = [bf(_dot(t16[c], bf(a_kb[c] * lower_left))) for c in chunks]
        tinv = [tinv[c] - _dot(left[c], t16[c]) for c in chunks]
        bs *= 2
    t16 = [bf(tinv[c]) for c in chunks]
    av = [bf(_dot(a_kk[c], vb[c])) for c in chunks]
    if wide:
        ku = [_dot(t16[c], jnp.concatenate([kap[c], av[c]], axis=1)) for c in chunks]
        kap2 = [bf(ku[c][:, :W]) for c in chunks]
        u0 = [bf(ku[c][:, W:]) for c in chunks]
    else:
        kap2 = [bf(_dot(t16[c], kap[c])) for c in chunks]
        u0 = [bf(_dot(t16[c], av[c])) for c in chunks]
    gamma = [jnp.where(lane_row < hs, e_pos[c][C - 1:C, :], e_pos[c][R - 1:R, :]) for c in chunks]
    m_p = [(eye_w - _dot_tn(kap2[c], bt[c])) * gamma[c] for c in chunks]
    g_p = [_dot_tn(jnp.concatenate([vb[c], u0[c]], axis=0), jnp.concatenate([kt[c], -bt[c]], axis=0)) * gamma[c]
           for c in chunks]
    if wide:
        ru = [_dot(a_rb[c], jnp.concatenate([kap2[c], u0[c]], axis=1)) for c in chunks]
        rho2 = [rho[c] - ru[c][:, :W] for c in chunks]
        y0 = [_dot(a_rk[c], vb[c]) - ru[c][:, W:] for c in chunks]
    else:
        rho2 = [rho[c] - _dot(a_rb[c], kap2[c]) for c in chunks]
        y0 = [_dot(a_rk[c], vb[c]) - _dot(a_rb[c], u0[c]) for c in chunks]
    bonus_v = [jnp.sum(rs[c] * ks[c] * rk, axis=-1, keepdims=True) * vs[c] for c in chunks]

    rho2_hl = [jnp.concatenate(split2(rho2[c]), axis=0) for c in chunks]
    m_p_hl = [jnp.concatenate(split2(m_p[c]), axis=1) for c in chunks]
    s = s_s[...]
    for c in chunks:
        s_hl = jnp.concatenate(split2(s), axis=0)
        py = _dot_nt(rho2_hl[c], s_hl)
        ys = py[:R, :W] + (py[:R, W:] + py[R:, :W]) + y0[c]
        ps = _dot(s_hl, m_p_hl[c])
        s = ps[:W, :W] + (ps[:W, W:] + ps[W:, :W]) + g_p[c]
        mean = jnp.sum(ys, axis=-1, keepdims=True) / hs
        d = (ys - mean) * headmask
        var = jnp.sum(d * d, axis=-1, keepdims=True) / hs
        yn = (d * lax.rsqrt(var + GN_EPS_B) * gng + gnb) * headmask + bonus_v[c]
        y_ref[rows[c], :] = ((yn[:C] + yn[C:]) * g_ref[rows[c], :]).astype(y_ref.dtype)
    s_s[...] = s

    @pl.when(tb == pl.num_programs(2) - 1)
    def _():
        s1_ref[0, 0] = s


def _rwkv(r, ld, k, v, kk, a, g, r_k, gn_g, gn_b, s0, batch, out_dtype):
    bt, wb = r.shape
    t = bt // batch
    nh, hs = s0.shape[1], s0.shape[2]
    assert 2 * hs == LANES and nh % 2 == 0
    npair = nh // 2
    chunk = math.gcd(t, CHUNK_B)
    tblk = _tile(t, (512, 256, 128, 64))
    ntb = t // tblk
    s0p = s0.reshape(batch, npair, 2, hs, hs)
    zero = jnp.zeros_like(s0p[:, :, 0])
    s0bd = jnp.concatenate([jnp.concatenate([s0p[:, :, 0], zero], axis=-1),
                            jnp.concatenate([zero, s0p[:, :, 1]], axis=-1)], axis=-2)
    blk = pl.BlockSpec((tblk, LANES), lambda b, p, i: (b * ntb + i, p))
    vec = pl.BlockSpec((1, LANES), lambda b, p, i: (0, p))
    st = pl.BlockSpec((1, 1, LANES, LANES), lambda b, p, i: (b, p, 0, 0))
    y, s1bd = pl.pallas_call(
        functools.partial(_rwkv_kernel, chunk=chunk, n_chunks=tblk // chunk, hs=hs),
        grid=(batch, npair, ntb),
        in_specs=[blk] * 7 + [vec] * 3 + [st],
        out_specs=[blk, st],
        out_shape=[jax.ShapeDtypeStruct((bt, wb), out_dtype),
                   jax.ShapeDtypeStruct((batch, npair, LANES, LANES), F32)],
        scratch_shapes=[pltpu.VMEM((LANES, LANES), F32)],
        compiler_params=_params(3),
        name="rwkv",
    )(r, ld, k, v, kk, a, g, r_k.reshape(1, wb), gn_g.reshape(1, wb), gn_b.reshape(1, wb), s0bd)
    s1 = jnp.stack([s1bd[:, :, :hs, :hs], s1bd[:, :, hs:, hs:]], axis=2).reshape(batch, nh, hs, hs)
    return y, s1


def _lf_cum_kernel(z_ref, b_ref, lf_ref, g_ref, carry_s, *, nh):
    @pl.when(pl.program_id(1) == 0)
    def _():
        carry_s[...] = jnp.zeros_like(carry_s)

    z = z_ref[...]
    lane = lax.broadcasted_iota(jnp.int32, z.shape, 1)
    lf = jnp.where(lane < nh, _log_sigmoid(jnp.where(lane < nh, z, 0.0) + b_ref[...]), 0.0)
    rows = lf.shape[0]
    ti = lax.broadcasted_iota(jnp.int32, (rows, rows), 0)
    si = lax.broadcasted_iota(jnp.int32, (rows, rows), 1)
    cum = jnp.dot((si <= ti).astype(F32), lf, precision=HIGHEST, preferred_element_type=F32) + carry_s[...]
    lf_ref[...] = lf
    g_ref[...] = cum
    carry_s[...] = cum[rows - 1:rows, :]


def _lf_cum(zf, b_f, batch):
    bt = zf.shape[0]
    t = bt // batch
    tb = _tile(t, (256, 128, 64, 32, 16, 8))
    ntb = t // tb
    bpad = jnp.zeros((1, LANES), F32).at[0, :b_f.shape[0]].set(b_f)
    blk = pl.BlockSpec((tb, LANES), lambda b, i: (b * ntb + i, 0))
    return pl.pallas_call(
        functools.partial(_lf_cum_kernel, nh=b_f.shape[0]),
        grid=(batch, ntb),
        in_specs=[blk, pl.BlockSpec((1, LANES), lambda b, i: (0, 0))],
        out_specs=[blk, blk],
        out_shape=[jax.ShapeDtypeStruct((bt, LANES), F32)] * 2,
        scratch_shapes=[pltpu.VMEM((1, LANES), F32)],
        compiler_params=_params(2),
        name="lf_cum",
    )(zf, bpad)


def _fox_kernel(q_ref, k_ref, v_ref, gq_ref, gk_ref, o_ref, kb_s, vb_s, *, bq, nq, scale):
    kb_s[...] = k_ref[...].astype(BF16)
    vb_s[...] = v_ref[...].astype(BF16)
    ti = lax.broadcasted_iota(jnp.int32, (bq, bq), 0)
    si = lax.broadcasted_iota(jnp.int32, (bq, bq), 1)

    for c in range(nq):
        past = c * bq
        rows = slice(past, past + bq)
        q = q_ref[rows, :].astype(BF16)
        gq = gq_ref[0, 0, rows, :]
        sd = _dot_nt(q, kb_s[rows, :]) * scale + (gq - gk_ref[0, 0, :, rows])
        sd = jnp.where(si <= ti, sd, NEG_INF)
        m = jnp.max(sd, axis=-1, keepdims=True)
        if past:
            sp = _dot_nt(q, kb_s[0:past, :]) * scale + (gq - gk_ref[0, 0, :, 0:past])
            m = jnp.maximum(m, jnp.max(sp, axis=-1, keepdims=True))
        pd = jnp.exp(sd - m)
        l = jnp.sum(pd, axis=-1, keepdims=True)
        acc = _dot(pd.astype(BF16), vb_s[rows, :])
        if past:
            pp = jnp.exp(sp - m)
            l = l + jnp.sum(pp, axis=-1, keepdims=True)
            acc = acc + _dot(pp.astype(BF16), vb_s[0:past, :])
        o_ref[rows, :] = (acc / l).astype(o_ref.dtype)


def _fox_prompt(q, k, v, slot, g, batch, nh):
    bt, width = q.shape
    t = bt // batch
    dh = width // nh
    bq = _tile(t, (256, 128))
    nq = t // bq
    gq = jnp.transpose(g, (0, 2, 1)).reshape(batch, nh, t, 1)
    gk = jnp.transpose(g, (0, 2, 1)).reshape(batch, nh, 1, t)
    return pl.pallas_call(
        functools.partial(_fox_kernel, bq=bq, nq=nq, scale=dh ** -0.5),
        grid=(batch, nh),
        in_specs=[pl.BlockSpec((t, dh), lambda b, h: (b, h)),
                  pl.BlockSpec((None, t, dh), lambda b, h: (slot, b, h)),
                  pl.BlockSpec((None, t, dh), lambda b, h: (slot, b, h)),
                  pl.BlockSpec((1, 1, t, 1), lambda b, h: (b, h, 0, 0)),
                  pl.BlockSpec((1, 1, 1, t), lambda b, h: (b, h, 0, 0))],
        out_specs=pl.BlockSpec((t, dh), lambda b, h: (b, h)),
        out_shape=jax.ShapeDtypeStruct((bt, width), BF16),
        scratch_shapes=[pltpu.VMEM((t, dh), BF16), pltpu.VMEM((t, dh), BF16)],
        compiler_params=_params(2),
        name="fox_prompt",
    )(q, k, v, gq, gk)


def _gpast_kernel(pt_ref, *refs, n_pages):
    del pt_ref
    lf_refs, o_ref, carry_s = refs[:n_pages], refs[n_pages], refs[n_pages + 1]

    @pl.when(pl.program_id(1) == 0)
    def _():
        carry_s[...] = jnp.zeros_like(carry_s)

    rows = lf_refs[0].shape[2]
    ti = lax.broadcasted_iota(jnp.int32, (rows, rows), 0)
    si = lax.broadcasted_iota(jnp.int32, (rows, rows), 1)
    later_rows = (si > ti).astype(F32)
    carry = carry_s[...]
    for k in range(n_pages):
        lf = lf_refs[k][0, 0].astype(F32)
        later = jnp.dot(later_rows, lf, precision=HIGHEST, preferred_element_type=F32)
        o_ref[0, n_pages - 1 - k] = -(later + carry)
        carry = carry + jnp.sum(lf, axis=0, keepdims=True)
    carry_s[...] = carry


def _gpast(cache_logf, layer, page_table):
    batch, npg = page_table.shape
    _, _, page, nh = cache_logf.shape
    pps = _tile(npg, (8, 4, 2))
    nsteps = npg // pps

    def page_spec(k):
        return pl.BlockSpec((1, 1, page, nh), lambda b, p, pt: (layer, pt[b, npg - 1 - (p * pps + k)], 0, 0))

    grid_spec = pltpu.PrefetchScalarGridSpec(
        num_scalar_prefetch=1,
        grid=(batch, nsteps),
        in_specs=[page_spec(k) for k in range(pps)],
        out_specs=pl.BlockSpec((1, pps, page, nh), lambda b, p, pt: (b, nsteps - 1 - p, 0, 0)),
        scratch_shapes=[pltpu.VMEM((1, nh), F32)],
    )
    return pl.pallas_call(
        functools.partial(_gpast_kernel, n_pages=pps),
        grid_spec=grid_spec,
        out_shape=jax.ShapeDtypeStruct((batch, npg, page, nh), F32),
        compiler_params=_params(2),
        name="gpast",
    )(page_table, *([cache_logf] * pps))


HEAD_GROUP = 8


def _fox_dec_kernel(pt_ref, q_ref, kp_ref, vp_ref, gp_ref, kn_ref, vn_ref, gq_ref, gn_ref, o_ref,
                    base_s, m_s, l_s, acc_s, *, scale):
    del pt_ref
    p = pl.program_id(1)
    hg = HEAD_GROUP
    ng, nq, dh = q_ref.shape[1:]
    nk = kp_ref.shape[2] * hg

    @pl.when(p == 0)
    def _():
        qi = lax.broadcasted_iota(jnp.int32, (nq, nk), 0)
        ki = lax.broadcasted_iota(jnp.int32, (nq, nk), 1)
        same_head = _mod_pow2(qi, hg) == _mod_pow2(ki, hg)
        for g in range(ng):
            base_s[g] = jnp.where(same_head, gq_ref[0, g], NEG_INF)
        m_s[...] = jnp.full_like(m_s, NEG_INF)
        l_s[...] = jnp.zeros_like(l_s)
        acc_s[...] = jnp.zeros_like(acc_s)

    groups = range(ng)

    def update(s, v):
        m_old = [m_s[g] for g in groups]
        m_new = [jnp.maximum(m_old[g], jnp.max(s[g], axis=-1, keepdims=True)) for g in groups]
        pr = [jnp.exp(s[g] - m_new[g]) for g in groups]
        pv = [_dot(pr[g].astype(BF16), v[g]) for g in groups]
        for g in groups:
            alpha = jnp.exp(m_old[g] - m_new[g])
            l_s[g] = alpha * l_s[g] + jnp.sum(pr[g], axis=-1, keepdims=True)
            acc_s[g] = alpha * acc_s[g] + pv[g]
            m_s[g] = m_new[g]

    heads = [slice(g * hg, (g + 1) * hg) for g in groups]
    q = [q_ref[0, g].astype(BF16) for g in groups]
    k = [kp_ref[0, 0, :, heads[g], :].reshape(nk, dh).astype(BF16) for g in groups]
    v = [vp_ref[0, 0, :, heads[g], :].reshape(nk, dh).astype(BF16) for g in groups]
    update([_dot_nt(q[g], k[g]) * scale + (base_s[g] - gp_ref[0, 0, g]) for g in groups], v)

    @pl.when(p == pl.num_programs(1) - 1)
    def _():
        qi = lax.broadcasted_iota(jnp.int32, (nq, nq), 0)
        ki = lax.broadcasted_iota(jnp.int32, (nq, nq), 1)
        keep = (_mod_pow2(qi, hg) == _mod_pow2(ki, hg)) & (_div_pow2(ki, hg) <= _div_pow2(qi, hg))
        s = [_dot_nt(q[g], kn_ref[0, g].astype(BF16)) * scale + (gq_ref[0, g] - gn_ref[0, g]) for g in groups]
        update([jnp.where(keep, s[g], NEG_INF) for g in groups], [vn_ref[0, g].astype(BF16) for g in groups])
        for g in groups:
            o_ref[0, g] = (acc_s[g] / l_s[g]).astype(o_ref.dtype)


def _fox_decode(q, kn, vn, g_new, cache_k, cache_v, g_past, layer, page_table):
    batch, ts, nh, dh = q.shape
    npg = page_table.shape[1]
    page = cache_k.shape[2]
    hg = HEAD_GROUP
    assert nh % hg == 0
    ng = nh // hg
    rows = ts * hg

    def grouped(x):
        n = x.shape[-1]
        return jnp.transpose(x.reshape(batch, ts, ng, hg, n), (0, 2, 1, 3, 4)).reshape(batch, ng, rows, n)

    gq = grouped(g_new[..., None])
    gn = gq.reshape(batch, ng, 1, rows)
    gp = jnp.transpose(g_past.reshape(batch, npg, page, ng, hg), (0, 1, 3, 2, 4)).reshape(batch, npg, ng, 1, page * hg)
    new = lambda r, n: pl.BlockSpec((1, ng, r, n), lambda b, p, pt: (b, 0, 0, 0))
    pg = pl.BlockSpec((1, 1, page, nh, dh), lambda b, p, pt: (layer, pt[b, p], 0, 0, 0))
    grid_spec = pltpu.PrefetchScalarGridSpec(
        num_scalar_prefetch=1,
        grid=(batch, npg),
        in_specs=[new(rows, dh), pg, pg,
                  pl.BlockSpec((1, 1, ng, 1, page * hg), lambda b, p, pt: (b, p, 0, 0, 0)),
                  new(rows, dh), new(rows, dh), new(rows, 1), new(1, rows)],
        out_specs=new(rows, dh),
        scratch_shapes=[pltpu.VMEM((ng, rows, page * hg), F32), pltpu.VMEM((ng, rows, 1), F32),
                        pltpu.VMEM((ng, rows, 1), F32), pltpu.VMEM((ng, rows, dh), F32)],
    )
    out = pl.pallas_call(
        functools.partial(_fox_dec_kernel, scale=dh ** -0.5),
        grid_spec=grid_spec,
        out_shape=jax.ShapeDtypeStruct((batch, ng, rows, dh), F32),
        compiler_params=_params(2),
        name="fox_decode",
    )(page_table, grouped(q), cache_k, cache_v, gp, grouped(kn), grouped(vn), gq, gn)
    return jnp.transpose(out.reshape(batch, ng, ts, hg, dh), (0, 2, 1, 3, 4)).reshape(batch, ts, nh, dh)


def _even_mixers(z_a, z_rest, batch, st, w, act):
    c0, n0, m0, s0, shift0 = st
    nh_a = c0.shape[1]
    t = z_a.shape[0] // batch
    off = 2 * nh_a
    gates = _mlstm_gates(z_rest, w["b_gate"], nh_a)
    h, c1, n1, m1 = _mlstm(z_a, gates, w["g_head"], c0, n0, m0, batch, act)
    r, ld, k, v, kk, a, g = _rwkv_prep(z_rest, off, shift0, w["mu"], w["w0"], w["w2"], w["a0"], w["a2"], w["g2"],
                                       w["kk_scale"], w["k_a"], batch)
    y, s1 = _rwkv(r, ld, k, v, kk, a, g, w["r_k"], w["gn_g"], w["gn_b"], s0, batch, act)
    shift1 = z_rest.reshape(batch, t, -1)[:, -1, off:off + shift0.shape[1]]
    return h.astype(BF16), y.astype(BF16), (c1, n1, m1, s1, shift1)


def _post_block(xp, xs, mix_p, mix_s, pp, ps, w, layer, alpha):
    x1p, x1pb = _add_ln(xp, mix_p, w["ln_mix_g"], w["ln_mix_b"], alpha)
    x1s, x1sb = _add_ln(xs, mix_s, w["ln_mix_g"], w["ln_mix_b"], alpha)
    hp, hs = _swiglu(x1pb, x1sb, w["w_ffn_in"], layer)
    x2p, x2pb = _add_ln(x1p, _mm(hp, w["w_ffn_out"], layer, F32), w["ln_ffn_g"], w["ln_ffn_b"], alpha)
    x2s, x2sb = _add_ln(x1s, _mm(hs, w["w_ffn_out"], layer, F32), w["ln_ffn_g"], w["ln_ffn_b"], alpha)
    return _ple(x2p, x2pb, pp, x2s, x2sb, ps, w["w_ple_gate"], w["w_ple"], layer)


def kernel(x_prompt, x_sample, cache_k, cache_v, cache_logf, state_mlstm_c, state_mlstm_n, state_mlstm_m, state_rwkv_wkv, state_rwkv_shift, page_table, p_prompt, p_sample, w_in_ab, b_gate_a, g_head_a, mu_b, w0_b, w2_b, a0_b, a2_b, g2_b, kk_scale_b, ka_b, rk_b, gn_g_b, gn_b_b, w_out_ab, w_in_c, b_f_c, w_out_c, ln_mix_g, ln_mix_b, ln_ffn_g, ln_ffn_b, w_ffn_in, w_ffn_out, w_ple, w_ple_gate):
    bp, tp, d = x_prompt.shape
    bs, ts, _ = x_sample.shape
    depth = ln_mix_g.shape[0]
    alpha = (2 * depth) ** 0.25
    nh_a, dh_a = state_mlstm_c.shape[2], state_mlstm_c.shape[3]
    wa = nh_a * dh_a
    nh_b, hs_b = state_rwkv_wkv.shape[2], state_rwkv_wkv.shape[3]
    wb = nh_b * hs_b
    lw, la, lg = w2_b.shape[1], a2_b.shape[1], g2_b.shape[1]
    nh_c, dh_c = cache_k.shape[3], cache_k.shape[4]
    n_in_a = 4 * wa + 2 * nh_a
    assert lw <= LANES and la <= LANES and 2 * nh_a <= LANES and nh_c <= LANES

    xp, xs = x_prompt.reshape(bp * tp, d), x_sample.reshape(bs * ts, d)
    xpb, xsb = xp.astype(BF16), xs.astype(BF16)
    pp = p_prompt.reshape(depth, bp * tp, -1).astype(BF16)
    ps = p_sample.reshape(depth, bs * ts, -1).astype(BF16)

    outs = {name: [] for name in ("lfp", "ks", "vs", "lfs", "cp", "np", "mp", "cs", "ns", "ms",
                                  "sp", "shp", "ss", "shs")}
    assert depth >= 2
    kp_all = vp_all = None
    w_ffn_out_b = w_ffn_out.astype(BF16)
    w_ab_t = jnp.swapaxes(w_in_ab, 1, 2)
    w_c_t = jnp.swapaxes(w_in_c, 1, 2)
    for i in range(depth):
        j = i // 2
        w = {"ln_mix_g": ln_mix_g[i], "ln_mix_b": ln_mix_b[i], "ln_ffn_g": ln_ffn_g[i], "ln_ffn_b": ln_ffn_b[i],
             "w_ffn_in": w_ffn_in, "w_ffn_out": w_ffn_out_b, "w_ple": w_ple, "w_ple_gate": w_ple_gate}
        if i % 2 == 0:
            w.update({"b_gate": b_gate_a[j], "g_head": g_head_a[j], "mu": mu_b[j], "w0": w0_b[j], "w2": w2_b[j],
                      "a0": a0_b[j], "a2": a2_b[j], "g2": g2_b[j], "kk_scale": kk_scale_b[j], "k_a": ka_b[j],
                      "r_k": rk_b[j].reshape(-1), "gn_g": gn_g_b[j], "gn_b": gn_b_b[j]})
            zero_state = (jnp.zeros((bp, nh_a, dh_a, dh_a), F32), jnp.zeros((bp, nh_a, dh_a), F32),
                          jnp.zeros((bp, nh_a), F32), jnp.zeros((bp, nh_b, hs_b, hs_b), F32),
                          jnp.zeros((bp, mu_b.shape[1]), F32))
            sample_state = (state_mlstm_c[j], state_mlstm_n[j], state_mlstm_m[j], state_rwkv_wkv[j],
                            state_rwkv_shift[j])
            n_rest = -(-(w_in_ab.shape[2] - 4 * wa) // LANES) * LANES
            za_p, za_s = _proj(xpb, xsb, w_ab_t, j, 0, 4 * wa, BF16, transposed=True)
            zr_p, zr_s = _proj(xpb, xsb, w_ab_t, j, 4 * wa, n_rest, F32, transposed=True)
            hp, yp, (cp, np_, mp, sp, shp) = _even_mixers(za_p, zr_p, bp, zero_state, w, BF16)
            hs, ys, (cs, ns, ms, ss, shs) = _even_mixers(za_s, zr_s, bs, sample_state, w, F32)
            mix_p, mix_s = _proj2(hp, yp, hs, ys, w_out_ab, j)
            for name, val in (("cp", cp), ("np", np_), ("mp", mp), ("sp", sp), ("shp", shp),
                              ("cs", cs), ("ns", ns), ("ms", ms), ("ss", ss), ("shs", shs)):
                outs[name].append(val)
        else:
            n_c = depth // 2
            qp, qs = _proj(xpb, xsb, w_c_t, j, 0, d, BF16, transposed=True)
            kp_all, kn = _proj(xpb, xsb, w_c_t, j, d, d, F32, stacked=(kp_all, j, n_c), transposed=True)
            vp_all, vn = _proj(xpb, xsb, w_c_t, j, 2 * d, d, F32, stacked=(vp_all, j, n_c), transposed=True)
            zfp, zfs = _proj(xpb, xsb, w_c_t, j, 3 * d, LANES, F32, transposed=True)
            lfp, gp = _lf_cum(zfp, b_f_c[j], bp)
            lfp = lfp[:, :nh_c].reshape(bp, tp, nh_c)
            att_p = _fox_prompt(qp, kp_all, vp_all, j, gp[:, :nh_c].reshape(bp, tp, nh_c), bp, nh_c)
            lfs, g_new = _lf_cum(zfs, b_f_c[j], bs)
            lfs = lfs[:, :nh_c].reshape(bs, ts, nh_c)
            g_past = _gpast(cache_logf, j, page_table)
            heads = lambda x: x.reshape(bs, ts, nh_c, dh_c)
            att_s = _fox_decode(heads(qs), heads(kn), heads(vn), g_new[:, :nh_c].reshape(bs, ts, nh_c),
                                cache_k, cache_v, g_past, j, page_table)
            mix_p, mix_s = _proj(att_p, att_s.reshape(bs * ts, d).astype(BF16), w_out_c, j, 0, d, F32)
            for name, val in (("lfp", lfp), ("ks", kn.reshape(bs, ts, nh_c, dh_c)),
                              ("vs", vn.reshape(bs, ts, nh_c, dh_c)), ("lfs", lfs)):
                outs[name].append(val)
        xp, xpb, xs, xsb = _post_block(xp, xs, mix_p, mix_s, pp[i], ps[i], w, i, alpha)
    stk = lambda name: jnp.stack(outs[name])
    return (xp.reshape(bp, tp, d), xs.reshape(bs, ts, d),
            kp_all.reshape(-1, bp, tp, nh_c, dh_c), vp_all.reshape(-1, bp, tp, nh_c, dh_c),
            stk("lfp"), stk("ks"), stk("vs"), stk("lfs"),
            stk("cp"), stk("np"), stk("mp"), stk("cs"), stk("ns"), stk("ms"),
            stk("sp"), stk("shp"), stk("ss"), stk("shs"))
```
